```python
import math
import jax, jax.numpy as jnp
from jax import lax
import numpy as np

D_MODEL = 1024
BATCH = 8
SEQ = 2048
DEPTH = 4
DEC_BATCH = 128
DEC_SEQ = 8
PAST_LEN = 8192
PAGE_SIZE = 128

N_MIXERS = 2
N_MLSTM = (DEPTH + 1) // 2
N_SWA = DEPTH // 2
MLSTM_HEADS = 4
MLSTM_DV = D_MODEL // MLSTM_HEADS
MLSTM_DK = MLSTM_DV // 2
MLSTM_CHUNK = 64
SWA_HEAD_DIM = 64
SWA_HEADS = D_MODEL // SWA_HEAD_DIM
SWA_KV_HEADS = 4
SWA_GROUP = SWA_HEADS // SWA_KV_HEADS
WINDOW = 128
ROT_DIM = SWA_HEAD_DIM // 4
ROPE_THETA = 500000.0
D_FF = 2816
CONV_W = 3
EPS = 1e-6

kernel_name = 'hybrid_mlstm_swa_convglu_step'


def rmsnorm(x, w):
    xf = x.astype(jnp.float32)
    y = xf * lax.rsqrt(jnp.mean(xf * xf, axis=-1, keepdims=True) + EPS)
    return (y * w.astype(jnp.float32)).astype(x.dtype)


def rope_partial(x, pos):
    half = ROT_DIM // 2
    inv = ROPE_THETA ** (-jnp.arange(half, dtype=jnp.float32) * 2.0 / ROT_DIM)
    ang = pos[:, None] * inv[None, :]
    cos = jnp.cos(ang)[:, None, :]
    sin = jnp.sin(ang)[:, None, :]
    x1 = x[..., :half]
    x2 = x[..., half:ROT_DIM]
    return jnp.concatenate([x1 * cos - x2 * sin, x2 * cos + x1 * sin, x[..., ROT_DIM:]], axis=-1)


def mlstm_chunkwise(q, k, v, logi, logf, C0, n0, m0):
    B, T, H, _ = q.shape
    DV = v.shape[-1]
    L = math.gcd(T, MLSTM_CHUNK)
    nc = T // L

    def to_chunks(x):
        x = x.astype(jnp.float32).reshape((B, nc, L, H) + x.shape[3:])
        return jnp.moveaxis(x, (1, 3), (0, 2))

    causal = jnp.tril(jnp.ones((L, L), dtype=bool))

    def step(carry, inp):
        C, n, m = carry
        qc, kc, vc, ic, fc = inp
        b = jnp.cumsum(fc, axis=-1)
        d = b[..., :, None] - b[..., None, :] + ic[..., None, :]
        d = jnp.where(causal, d, -jnp.inf)
        m_inter = b + m[..., None]
        m_t = jnp.maximum(m_inter, d.max(-1))
        w_intra = jnp.exp(d - m_t[..., None]) * jnp.einsum('bhtd,bhsd->bhts', qc, kc)
        w_inter = jnp.exp(m_inter - m_t)
        num = w_inter[..., None] * jnp.einsum('bhtd,bhde->bhte', qc, C) + jnp.einsum('bhts,bhse->bhte', w_intra, vc)
        nq = w_inter * jnp.einsum('bhtd,bhd->bht', qc, n) + w_intra.sum(-1)
        h = num / jnp.maximum(jnp.abs(nq), jnp.exp(-m_t))[..., None]
        m_new = m_t[..., -1]
        w_state = jnp.exp(b[..., -1:] - b + ic - m_new[..., None])
        decay = jnp.exp(b[..., -1] + m - m_new)
        C_new = decay[..., None, None] * C + jnp.einsum('bhs,bhsd,bhse->bhde', w_state, kc, vc)
        n_new = decay[..., None] * n + jnp.einsum('bhs,bhsd->bhd', w_state, kc)
        return (C_new, n_new, m_new), h

    carry0 = (C0.astype(jnp.float32), n0.astype(jnp.float32), m0.astype(jnp.float32))
    (C, n, m), hs = lax.scan(step, carry0, (to_chunks(q), to_chunks(k), to_chunks(v), to_chunks(logi), to_chunks(logf)))
    hs = hs.transpose(1, 0, 3, 2, 4).reshape(B, T, H, DV)
    return hs, C, n, m


def mlstm_mixer(h, C0, n0, m0, w_in, b_gates, head_norm, w_out):
    B, T, _ = h.shape
    HK = MLSTM_HEADS * MLSTM_DK
    HV = MLSTM_HEADS * MLSTM_DV
    proj = h @ w_in
    q = proj[..., :HK].reshape(B, T, MLSTM_HEADS, MLSTM_DK)
    k = proj[..., HK:2 * HK].reshape(B, T, MLSTM_HEADS, MLSTM_DK) * (MLSTM_DK ** -0.5)
    v = proj[..., 2 * HK:2 * HK + HV].reshape(B, T, MLSTM_HEADS, MLSTM_DV)
    o = proj[..., 2 * HK + HV:2 * HK + 2 * HV]
    g = proj[..., 2 * HK + 2 * HV:].astype(jnp.float32) + b_gates.astype(jnp.float32)
    logi = g[..., :MLSTM_HEADS]
    logf = jax.nn.log_sigmoid(g[..., MLSTM_HEADS:])
    hs, C, n, m = mlstm_chunkwise(q, k, v, logi, logf, C0, n0, m0)
    hn = hs * lax.rsqrt(jnp.mean(hs * hs, axis=-1, keepdims=True) + EPS)
    hn = hn.reshape(B, T, HV) * head_norm.astype(jnp.float32)
    y = (hn.astype(h.dtype) * jax.nn.sigmoid(o)) @ w_out
    return y, C.astype(C0.dtype), n.astype(n0.dtype), m.astype(m0.dtype)


def swa_qkv(h, pos, w_qkv, q_norm, k_norm):
    B, T, _ = h.shape
    HQ = SWA_HEADS * SWA_HEAD_DIM
    HKV = SWA_KV_HEADS * SWA_HEAD_DIM
    proj = h @ w_qkv
    q = proj[..., :HQ].reshape(B, T, SWA_HEADS, SWA_HEAD_DIM)
    k = proj[..., HQ:HQ + HKV].reshape(B, T, SWA_KV_HEADS, SWA_HEAD_DIM)
    v = proj[..., HQ + HKV:].reshape(B, T, SWA_KV_HEADS, SWA_HEAD_DIM)
    q = rope_partial(rmsnorm(q, q_norm).astype(jnp.float32), pos).astype(h.dtype)
    k = rope_partial(rmsnorm(k, k_norm).astype(jnp.float32), pos).astype(h.dtype)
    return q, k, v


def swa_attend(q, k, v, mask, sinks):
    qg = q.reshape(q.shape[:-2] + (SWA_KV_HEADS, SWA_GROUP, SWA_HEAD_DIM))
    s = jnp.einsum('...qkgd,...skd->...kgqs', qg, k, preferred_element_type=jnp.float32) * (SWA_HEAD_DIM ** -0.5)
    s = jnp.where(mask[..., None, None, :, :], s, -jnp.inf)
    sink = sinks.astype(jnp.float32).reshape(SWA_KV_HEADS, SWA_GROUP, 1, 1)
    m = jnp.maximum(s.max(-1, keepdims=True), sink)
    p = jnp.exp(s - m)
    denom = p.sum(-1, keepdims=True) + jnp.exp(sink - m)
    o = jnp.einsum('...kgqs,...skd->...qkgd', (p / denom).astype(v.dtype), v)
    return o.reshape(q.shape)


def swa_prompt(h, w_qkv, q_norm, k_norm, sinks, w_out):
    B, T, _ = h.shape
    q, k, v = swa_qkv(h, jnp.arange(T, dtype=jnp.float32), w_qkv, q_norm, k_norm)
    nb = T // WINDOW
    qb = q.reshape(B, nb, WINDOW, SWA_HEADS, SWA_HEAD_DIM)

    def band(x):
        xb = x.reshape(B, nb, WINDOW, SWA_KV_HEADS, SWA_HEAD_DIM)
        prev = jnp.concatenate([jnp.zeros_like(xb[:, :1]), xb[:, :-1]], axis=1)
        return jnp.concatenate([prev, xb], axis=2)

    blk = jnp.arange(nb)[:, None]
    qpos = blk * WINDOW + jnp.arange(WINDOW)[None, :]
    kpos = (blk - 1) * WINDOW + jnp.arange(2 * WINDOW)[None, :]
    rel = qpos[:, :, None] - kpos[:, None, :]
    mask = (rel >= 0) & (rel <= WINDOW) & (kpos[:, None, :] >= 0)
    o = swa_attend(qb, band(k), band(v), mask, sinks).reshape(B, T, SWA_HEADS * SWA_HEAD_DIM)
    return o @ w_out, k[:, T - WINDOW:], v[:, T - WINDOW:]


def swa_sample(h, k_buf, v_buf, w_qkv, q_norm, k_norm, sinks, w_out):
    B, T, _ = h.shape
    W = k_buf.shape[1]
    q, k, v = swa_qkv(h, PAST_LEN + jnp.arange(T, dtype=jnp.float32), w_qkv, q_norm, k_norm)
    k_all = jnp.concatenate([k_buf.astype(k.dtype), k], axis=1)
    v_all = jnp.concatenate([v_buf.astype(v.dtype), v], axis=1)
    qpos = PAST_LEN + jnp.arange(T)
    kpos = PAST_LEN - W + jnp.arange(W + T)
    rel = qpos[:, None] - kpos[None, :]
    mask = (rel >= 0) & (rel <= WINDOW)
    o = swa_attend(q, k_all, v_all, mask, sinks).reshape(B, T, SWA_HEADS * SWA_HEAD_DIM)
    return o @ w_out, k_all[:, T:], v_all[:, T:]


def conv_glu(h, conv_buf, w_up, conv_w, conv_b, w_down):
    B, T, _ = h.shape
    up = h @ w_up
    g = up[..., :D_FF]
    u = up[..., D_FF:]
    gp = jnp.concatenate([conv_buf.astype(g.dtype), g], axis=1)
    gc = conv_b
    for j in range(CONV_W):
        gc = gc + conv_w[j] * gp[:, j:j + T]
    y = (jax.nn.gelu(gc, approximate=False) * u) @ w_down
    return y, gp[:, T:]


def setup_inputs(seed: int = 0) -> dict:
    key = jax.random.key(seed)
    ks = jax.random.split(key, 24)
    f32 = jnp.float32
    nrm = lambda k, shape, s: jax.random.normal(k, shape, f32) * s
    w_buf = min(WINDOW, PAST_LEN)
    mlstm_in_width = 2 * MLSTM_HEADS * MLSTM_DK + 2 * MLSTM_HEADS * MLSTM_DV + 2 * MLSTM_HEADS
    swa_in_width = (SWA_HEADS + 2 * SWA_KV_HEADS) * SWA_HEAD_DIM
    b_gates = jnp.concatenate([
        nrm(ks[0], (N_MLSTM, MLSTM_HEADS), 0.1),
        3.0 + nrm(ks[1], (N_MLSTM, MLSTM_HEADS), 0.1)], axis=-1)
    return {
        'x_prompt': nrm(ks[2], (BATCH, SEQ, D_MODEL), 1.0),
        'x_sample': nrm(ks[3], (DEC_BATCH, DEC_SEQ, D_MODEL), 1.0),
        'state_mlstm_C': nrm(ks[4], (N_MLSTM, DEC_BATCH, MLSTM_HEADS, MLSTM_DK, MLSTM_DV), 0.1),
        'state_mlstm_n': nrm(ks[5], (N_MLSTM, DEC_BATCH, MLSTM_HEADS, MLSTM_DK), 0.5),
        'state_mlstm_m': nrm(ks[6], (N_MLSTM, DEC_BATCH, MLSTM_HEADS), 1.0),
        'cache_swa_k': nrm(ks[7], (N_SWA, DEC_BATCH, w_buf, SWA_KV_HEADS, SWA_HEAD_DIM), 1.0),
        'cache_swa_v': nrm(ks[8], (N_SWA, DEC_BATCH, w_buf, SWA_KV_HEADS, SWA_HEAD_DIM), 1.0),
        'state_ffn_conv': nrm(ks[9], (DEPTH, DEC_BATCH, CONV_W - 1, D_FF), 1.0),
        'norm_mix': 1.0 + nrm(ks[10], (DEPTH, D_MODEL), 0.02),
        'norm_ffn': 1.0 + nrm(ks[11], (DEPTH, D_MODEL), 0.02),
        'w_mlstm_in': nrm(ks[12], (N_MLSTM, D_MODEL, mlstm_in_width), D_MODEL ** -0.5),
        'b_mlstm_gates': b_gates,
        'mlstm_head_norm': 1.0 + nrm(ks[13], (N_MLSTM, MLSTM_HEADS * MLSTM_DV), 0.02),
        'w_mlstm_out': nrm(ks[14], (N_MLSTM, MLSTM_HEADS * MLSTM_DV, D_MODEL), (MLSTM_HEADS * MLSTM_DV) ** -0.5),
        'w_swa_qkv': nrm(ks[15], (N_SWA, D_MODEL, swa_in_width), D_MODEL ** -0.5),
        'swa_q_norm': 1.0 + nrm(ks[16], (N_SWA, SWA_HEAD_DIM), 0.02),
        'swa_k_norm': 1.0 + nrm(ks[17], (N_SWA, SWA_HEAD_DIM), 0.02),
        'swa_sinks': nrm(ks[18], (N_SWA, SWA_HEADS), 0.5),
        'w_swa_out': nrm(ks[19], (N_SWA, SWA_HEADS * SWA_HEAD_DIM, D_MODEL), (SWA_HEADS * SWA_HEAD_DIM) ** -0.5),
        'w_ffn_up': nrm(ks[20], (DEPTH, D_MODEL, 2 * D_FF), D_MODEL ** -0.5),
        'ffn_conv_w': nrm(ks[21], (DEPTH, CONV_W, D_FF), CONV_W ** -0.5),
        'ffn_conv_b': nrm(ks[22], (DEPTH, D_FF), 0.02),
        'w_ffn_down': nrm(ks[23], (DEPTH, D_FF, D_MODEL), D_FF ** -0.5),
    }


def reference(x_prompt, x_sample, state_mlstm_C, state_mlstm_n, state_mlstm_m, cache_swa_k, cache_swa_v,
              state_ffn_conv, norm_mix, norm_ffn, w_mlstm_in, b_mlstm_gates, mlstm_head_norm, w_mlstm_out,
              w_swa_qkv, swa_q_norm, swa_k_norm, swa_sinks, w_swa_out, w_ffn_up, ffn_conv_w, ffn_conv_b,
              w_ffn_down):
    xp, xs = x_prompt, x_sample
    B = xp.shape[0]
    dt = xp.dtype
    pC, pn, pm, pk, pv, pconv = [], [], [], [], [], []
    sC, sn, sm, sk, sv, sconv = [], [], [], [], [], []
    for i in range(DEPTH):
        j = i // N_MIXERS
        hp = rmsnorm(xp, norm_mix[i])
        hs = rmsnorm(xs, norm_mix[i])
        if i % N_MIXERS == 0:
            zC = jnp.zeros((B, MLSTM_HEADS, MLSTM_DK, MLSTM_DV), dt)
            zn = jnp.zeros((B, MLSTM_HEADS, MLSTM_DK), dt)
            zm = jnp.zeros((B, MLSTM_HEADS), dt)
            yp, C1, n1, m1 = mlstm_mixer(hp, zC, zn, zm, w_mlstm_in[j], b_mlstm_gates[j], mlstm_head_norm[j], w_mlstm_out[j])
            ys, C2, n2, m2 = mlstm_mixer(hs, state_mlstm_C[j], state_mlstm_n[j], state_mlstm_m[j],
                                         w_mlstm_in[j], b_mlstm_gates[j], mlstm_head_norm[j], w_mlstm_out[j])
            pC.append(C1); pn.append(n1); pm.append(m1)
            sC.append(C2); sn.append(n2); sm.append(m2)
        else:
            yp, k1, v1 = swa_prompt(hp, w_swa_qkv[j], swa_q_norm[j], swa_k_norm[j], swa_sinks[j], w_swa_out[j])
            ys, k2, v2 = swa_sample(hs, cache_swa_k[j], cache_swa_v[j], w_swa_qkv[j], swa_q_norm[j],
                                    swa_k_norm[j], swa_sinks[j], w_swa_out[j])
            pk.append(k1); pv.append(v1)
            sk.append(k2); sv.append(v2)
        xp = xp + yp
        xs = xs + ys
        hp = rmsnorm(xp, norm_ffn[i])
        hs = rmsnorm(xs, norm_ffn[i])
        zconv = jnp.zeros((B, CONV_W - 1, D_FF), dt)
        fp, c1 = conv_glu(hp, zconv, w_ffn_up[i], ffn_conv_w[i], ffn_conv_b[i], w_ffn_down[i])
        fs, c2 = conv_glu(hs, state_ffn_conv[i], w_ffn_up[i], ffn_conv_w[i], ffn_conv_b[i], w_ffn_down[i])
        pconv.append(c1); sconv.append(c2)
        xp = xp + fp
        xs = xs + fs
    return (xp, xs,
            jnp.stack(pC), jnp.stack(pn), jnp.stack(pm), jnp.stack(pk), jnp.stack(pv), jnp.stack(pconv),
            jnp.stack(sC), jnp.stack(sn), jnp.stack(sm), jnp.stack(sk), jnp.stack(sv), jnp.stack(sconv))
```

```python
import functools
import math

import jax
import jax.numpy as jnp
from jax import lax
from jax.experimental import pallas as pl
from jax.experimental.pallas import tpu as pltpu

F32 = jnp.float32
BF16 = jnp.bfloat16

D_MODEL = 1024
DEPTH = 4
PAST_LEN = 8192
M_HEADS = 4
M_DV = D_MODEL // M_HEADS
M_DK = M_DV // 2
M_HK = M_HEADS * M_DK
M_HV = M_HEADS * M_DV
M_PROJ = 2 * M_HK + 2 * M_HV
S_HD = 64
S_HEADS = D_MODEL // S_HD
S_KV = 4
S_GROUP = S_HEADS // S_KV
S_HQ = S_HEADS * S_HD
S_HKV = S_KV * S_HD
WINDOW = 128
ROT_DIM = S_HD // 4
ROPE_THETA = 500000.0
D_FF = 2816
CONV_W = 3
EPS = 1e-6

LANES = 128
SUBLANES = 8
VMEM_LIMIT = 56 * 1024 * 1024
TOKEN_TILE = 512
FF_CHUNK = 256
PROJ_CHUNK = 512
MLSTM_CHUNK = 256
SAMPLE_BLOCK = 8
NEG_INF = float("-inf")


def _params(*sem):
    return pltpu.CompilerParams(dimension_semantics=sem, vmem_limit_bytes=VMEM_LIMIT)


def _resident(shape):
    zeros = (0,) * len(shape)
    return pl.BlockSpec(shape, lambda *_: zeros, pipeline_mode=pl.Buffered(1))


def _rmsnorm(x, w):
    ms = jnp.mean(x * x, axis=-1, keepdims=True)
    return x * lax.rsqrt(ms + EPS) * w


def _dot(a, b):
    return jnp.dot(a, b, preferred_element_type=F32)


def _dot_nt(a, b):
    return lax.dot_general(a, b, (((1,), (1,)), ((), ())), preferred_element_type=F32)


def _dot_tn(a, b):
    return lax.dot_general(a, b, (((0,), (0,)), ((), ())), preferred_element_type=F32)


def _mlstm_inproj_kernel(x_ref, nw_ref, w_ref, wg_ref, bg_ref, proj_ref, gate_ref):
    h = _rmsnorm(x_ref[...], nw_ref[...])
    hb = h.astype(BF16)
    for j in range(M_PROJ // PROJ_CHUNK):
        cols = slice(j * PROJ_CHUNK, (j + 1) * PROJ_CHUNK)
        r = _dot(hb, w_ref[:, cols])
        if j * PROJ_CHUNK == M_HK:
            r = r * (M_DK ** -0.5)
        proj_ref[:, cols] = r.astype(proj_ref.dtype)
    g = jnp.dot(h, wg_ref[...], preferred_element_type=F32, precision=lax.Precision.HIGHEST) + bg_ref[...]
    logf = -(jnp.maximum(-g, 0.0) + jnp.log1p(jnp.exp(-jnp.abs(g))))
    lane = lax.broadcasted_iota(jnp.int32, g.shape, 1)
    gate_ref[...] = jnp.where(lane < M_HEADS, g, logf)


def _mlstm_inproj(x, nw, w, wg, bg, proj_dtype):
    n = x.shape[0]
    tm = TOKEN_TILE
    assert n % tm == 0 and M_HK == PROJ_CHUNK
    return pl.pallas_call(
        _mlstm_inproj_kernel,
        grid=(n // tm,),
        in_specs=[
            pl.BlockSpec((tm, D_MODEL), lambda i: (i, 0)),
            _resident((1, D_MODEL)),
            _resident((D_MODEL, M_PROJ)),
            _resident((D_MODEL, LANES)),
            _resident((1, LANES)),
        ],
        out_specs=[
            pl.BlockSpec((tm, M_PROJ), lambda i: (i, 0)),
            pl.BlockSpec((tm, LANES), lambda i: (i, 0)),
        ],
        out_shape=[
            jax.ShapeDtypeStruct((n, M_PROJ), proj_dtype),
            jax.ShapeDtypeStruct((n, LANES), F32),
        ],
        compiler_params=_params("parallel"),
        name="mlstm_inproj",
    )(x, nw, w, wg, bg)


def _mlstm_chunk(q, k, v, i_col, f_col, C, n, m):
    L = q.shape[0]
    t_idx = lax.broadcasted_iota(jnp.int32, (L, L), 0)
    s_idx = lax.broadcasted_iota(jnp.int32, (L, L), 1)
    causal = s_idx <= t_idx
    eye = s_idx == t_idx

    def to_row(col):
        return jnp.sum(jnp.where(eye, col, 0.0), axis=0, keepdims=True)

    f_row = to_row(f_col)
    i_row = to_row(i_col)
    b_col = jnp.sum(jnp.where(causal, f_row, 0.0), axis=1, keepdims=True)
    b_row = to_row(b_col)
    d = jnp.where(causal, b_col - b_row + i_row, NEG_INF)
    m_inter = b_col + m
    m_t = jnp.maximum(m_inter, jnp.max(d, axis=1, keepdims=True))
    w_intra = jnp.exp(d - m_t) * _dot_nt(q, k)
    w_inter = jnp.exp(m_inter - m_t)
    num = w_inter * _dot(q, C.astype(BF16)) + _dot(w_intra.astype(BF16), v)
    qn = jnp.sum(q.astype(F32) * n, axis=1, keepdims=True)
    nq = w_inter * qn + jnp.sum(w_intra, axis=1, keepdims=True)
    den = jnp.maximum(jnp.abs(nq), jnp.exp(-m_t))
    h = num * (1.0 / den)
    m_new = m_t[L - 1:L, :]
    b_last = b_col[L - 1:L, :]
    w_state = jnp.exp(b_last - b_col + i_col - m_new)
    decay = jnp.exp(b_last + m - m_new)
    kw = k.astype(F32) * w_state
    C_new = decay * C + _dot_tn(kw.astype(BF16), v)
    n_new = decay * n + jnp.sum(kw, axis=0, keepdims=True)
    return h, C_new, n_new, m_new


def _mlstm_prompt_kernel(q_ref, k_ref, v_ref, g_ref, hs_ref, C_ref, n_ref, m_ref, *, chunk):
    head = pl.program_id(1)
    T = q_ref.shape[1]
    C_ref[...] = jnp.zeros_like(C_ref)
    n_ref[...] = jnp.zeros_like(n_ref)
    m_ref[...] = jnp.zeros_like(m_ref)
    lane = lax.broadcasted_iota(jnp.int32, (chunk, LANES), 1)

    def body(c, carry):
        rows = pl.ds(pl.multiple_of(c * chunk, chunk), chunk)
        g = g_ref[0, rows, :]
        i_col = jnp.sum(jnp.where(lane == head, g, 0.0), axis=1, keepdims=True)
        f_col = jnp.sum(jnp.where(lane == head + M_HEADS, g, 0.0), axis=1, keepdims=True)
        h, C_new, n_new, m_new = _mlstm_chunk(
            q_ref[0, rows, :], k_ref[0, rows, :], v_ref[0, rows, :], i_col, f_col,
            C_ref[0, 0], n_ref[0, 0], m_ref[0, 0, :, 0:1])
        hs_ref[0, rows, :] = h.astype(hs_ref.dtype)
        C_ref[0, 0] = C_new
        n_ref[0, 0] = n_new
        m_ref[0, 0] = jnp.broadcast_to(m_new, (1, LANES))
        return carry

    lax.fori_loop(0, T // chunk, body, 0)


def _mlstm_prompt(proj, gates, B, T):
    chunk = MLSTM_CHUNK
    assert T % chunk == 0
    kq = M_HK // M_DK
    kv = 2 * M_HK // M_DV
    return pl.pallas_call(
        functools.partial(_mlstm_prompt_kernel, chunk=chunk),
        grid=(B, M_HEADS),
        in_specs=[
            pl.BlockSpec((1, T, M_DK), lambda b, h: (b, 0, h)),
            pl.BlockSpec((1, T, M_DK), lambda b, h: (b, 0, kq + h)),
            pl.BlockSpec((1, T, M_DV), lambda b, h: (b, 0, kv + h)),
            pl.BlockSpec((1, T, LANES), lambda b, h: (b, 0, 0)),
        ],
        out_specs=[
            pl.BlockSpec((1, T, M_DV), lambda b, h: (b, 0, h)),
            pl.BlockSpec((1, 1, M_DK, M_DV), lambda b, h: (b, h, 0, 0)),
            pl.BlockSpec((1, 1, 1, M_DK), lambda b, h: (b, h, 0, 0)),
            pl.BlockSpec((1, 1, 1, LANES), lambda b, h: (b, h, 0, 0)),
        ],
        out_shape=[
            jax.ShapeDtypeStruct((B, T, M_HV), BF16),
            jax.ShapeDtypeStruct((B, M_HEADS, M_DK, M_DV), F32),
            jax.ShapeDtypeStruct((B, M_HEADS, 1, M_DK), F32),
            jax.ShapeDtypeStruct((B, M_HEADS, 1, LANES), F32),
        ],
        compiler_params=_params("parallel", "parallel"),
        name="mlstm_prompt",
    )(proj, proj, proj, gates)


def _mlstm_sample_kernel(q_ref, k_ref, v_ref, g_ref, C0_ref, n0_ref, m0_ref,
                         hs_ref, C_ref, n_ref, m_ref, *, T):
    nb = C0_ref.shape[0]

    def body(b, carry):
        rows = pl.ds(pl.multiple_of(b * T, T), T)
        g = g_ref[rows, :]
        for hd in range(M_HEADS):
            h, C_new, n_new, m_new = _mlstm_chunk(
                q_ref[rows, hd * M_DK:(hd + 1) * M_DK].astype(BF16),
                k_ref[rows, hd * M_DK:(hd + 1) * M_DK].astype(BF16),
                v_ref[rows, hd * M_DV:(hd + 1) * M_DV].astype(BF16),
                g[:, hd:hd + 1], g[:, M_HEADS + hd:M_HEADS + hd + 1],
                C0_ref[b, hd], n0_ref[b, hd], m0_ref[b, hd, :, 0:1])
            hs_ref[rows, hd * M_DV:(hd + 1) * M_DV] = h.astype(hs_ref.dtype)
            C_ref[b, hd] = C_new
            n_ref[b, hd] = n_new
            m_ref[b, hd] = jnp.broadcast_to(m_new, (1, LANES))
        return carry

    lax.fori_loop(0, nb, body, 0)


def _mlstm_sample(proj, gates, C0, n0, m0, B, T):
    nb = SAMPLE_BLOCK
    assert B % nb == 0 and T % SUBLANES == 0
    rows = nb * T
    return pl.pallas_call(
        functools.partial(_mlstm_sample_kernel, T=T),
        grid=(B // nb,),
        in_specs=[
            pl.BlockSpec((rows, M_HK), lambda i: (i, 0)),
            pl.BlockSpec((rows, M_HK), lambda i: (i, 1)),
            pl.BlockSpec((rows, M_HV), lambda i: (i, 2 * M_HK // M_HV)),
            pl.BlockSpec((rows, LANES), lambda i: (i, 0)),
            pl.BlockSpec((nb, M_HEADS, M_DK, M_DV), lambda i: (i, 0, 0, 0)),
            pl.BlockSpec((nb, M_HEADS, 1, M_DK), lambda i: (i, 0, 0, 0)),
            pl.BlockSpec((nb, M_HEADS, 1, LANES), lambda i: (i, 0, 0, 0)),
        ],
        out_specs=[
            pl.BlockSpec((rows, M_HV), lambda i: (i, 0)),
            pl.BlockSpec((nb, M_HEADS, M_DK, M_DV), lambda i: (i, 0, 0, 0)),
            pl.BlockSpec((nb, M_HEADS, 1, M_DK), lambda i: (i, 0, 0, 0)),
            pl.BlockSpec((nb, M_HEADS, 1, LANES), lambda i: (i, 0, 0, 0)),
        ],
        out_shape=[
            jax.ShapeDtypeStruct((B * T, M_HV), BF16),
            jax.ShapeDtypeStruct((B, M_HEADS, M_DK, M_DV), F32),
            jax.ShapeDtypeStruct((B, M_HEADS, 1, M_DK), F32),
            jax.ShapeDtypeStruct((B, M_HEADS, 1, LANES), F32),
        ],
        compiler_params=_params("parallel"),
        name="mlstm_sample",
    )(proj, proj, proj, gates, C0, n0, m0)


def _mlstm_out_kernel(hs_ref, o_ref, hw_ref, w_ref, x_ref, y_ref):
    hs = hs_ref[...].astype(F32)
    parts = []
    for hd in range(M_HEADS):
        seg = hs[:, hd * M_DV:(hd + 1) * M_DV]
        ms = jnp.mean(seg * seg, axis=-1, keepdims=True)
        parts.append(seg * lax.rsqrt(ms + EPS))
    hn = jnp.concatenate(parts, axis=1) * hw_ref[...]
    a = hn * jax.nn.sigmoid(o_ref[...].astype(F32))
    y_ref[...] = x_ref[...] + _dot(a.astype(BF16), w_ref[...])


def _mlstm_out(hs, proj, hw, w, x):
    n = x.shape[0]
    tm = TOKEN_TILE
    o_block = (2 * M_HK + M_HV) // M_HV
    return pl.pallas_call(
        _mlstm_out_kernel,
        grid=(n // tm,),
        in_specs=[
            pl.BlockSpec((tm, M_HV), lambda i: (i, 0)),
            pl.BlockSpec((tm, M_HV), lambda i: (i, o_block)),
            _resident((1, M_HV)),
            _resident((M_HV, D_MODEL)),
            pl.BlockSpec((tm, D_MODEL), lambda i: (i, 0)),
        ],
        out_specs=pl.BlockSpec((tm, D_MODEL), lambda i: (i, 0)),
        out_shape=jax.ShapeDtypeStruct((n, D_MODEL), F32),
        compiler_params=_params("parallel"),
        name="mlstm_out",
    )(hs, proj, hw, w, x)


def _head_norm_rope(y, w, cos, s_lo, s_hi):
    lane = lax.broadcasted_iota(jnp.int32, y.shape, 1)
    first = lane < S_HD
    sq = y * y
    lo = jnp.sum(jnp.where(first, sq, 0.0), axis=-1, keepdims=True)
    hi = jnp.sum(jnp.where(first, 0.0, sq), axis=-1, keepdims=True)
    ms = jnp.where(first, lo, hi) * (1.0 / S_HD)
    z = y * lax.rsqrt(ms + EPS) * w
    half = ROT_DIM // 2
    return z * cos + pltpu.roll(z, LANES - half, 1) * s_lo + pltpu.roll(z, half, 1) * s_hi


def _swa_qkv_kernel(x_ref, nw_ref, w_ref, qw_ref, kw_ref, cos_ref, slo_ref, shi_ref,
                    q_ref, k_ref, v_ref):
    hb = _rmsnorm(x_ref[...], nw_ref[...]).astype(BF16)
    cos, s_lo, s_hi = cos_ref[...], slo_ref[...], shi_ref[...]
    for j in range(S_HQ // LANES):
        cols = slice(j * LANES, (j + 1) * LANES)
        y = _head_norm_rope(_dot(hb, w_ref[:, cols]), qw_ref[...], cos, s_lo, s_hi)
        q_ref[:, cols] = (y * (S_HD ** -0.5)).astype(q_ref.dtype)
    for j in range(S_HKV // LANES):
        cols = slice(j * LANES, (j + 1) * LANES)
        wcols = slice(S_HQ + j * LANES, S_HQ + (j + 1) * LANES)
        k_ref[:, cols] = _head_norm_rope(_dot(hb, w_ref[:, wcols]), kw_ref[...], cos, s_lo, s_hi)
    v_ref[...] = _dot(hb, w_ref[:, S_HQ + S_HKV:])


def _swa_qkv(x, nw, w, qw, kw, cos, s_lo, s_hi):
    n = x.shape[0]
    tm = TOKEN_TILE
    tab_blocks = cos.shape[0] // tm
    tab = pl.BlockSpec((tm, LANES), lambda i: (i % tab_blocks, 0))
    return pl.pallas_call(
        _swa_qkv_kernel,
        grid=(n // tm,),
        in_specs=[
            pl.BlockSpec((tm, D_MODEL), lambda i: (i, 0)),
            _resident((1, D_MODEL)),
            _resident((D_MODEL, S_HQ + 2 * S_HKV)),
            _resident((1, LANES)),
            _resident((1, LANES)),
            tab, tab, tab,
        ],
        out_specs=[
            pl.BlockSpec((tm, S_HQ), lambda i: (i, 0)),
            pl.BlockSpec((tm, S_HKV), lambda i: (i, 0)),
            pl.BlockSpec((tm, S_HKV), lambda i: (i, 0)),
        ],
        out_shape=[
            jax.ShapeDtypeStruct((n, S_HQ), BF16),
            jax.ShapeDtypeStruct((n, S_HKV), F32),
            jax.ShapeDtypeStruct((n, S_HKV), F32),
        ],
        compiler_params=_params("parallel"),
        name="swa_qkv",
    )(x, nw, w, qw, kw, cos, s_lo, s_hi)


def _rope_tables(pos):
    half = ROT_DIM // 2
    inv = ROPE_THETA ** (-jnp.arange(half, dtype=F32) * 2.0 / ROT_DIM)
    ang = pos[:, None] * inv[None, :]
    cos, sin = jnp.cos(ang), jnp.sin(ang)
    n = pos.shape[0]
    pad = jnp.zeros((n, S_HD - ROT_DIM), F32)
    zero = jnp.zeros((n, half), F32)
    cos_h = jnp.concatenate([cos, cos, pad + 1.0], axis=1)
    lo_h = jnp.concatenate([-sin, zero, pad], axis=1)
    hi_h = jnp.concatenate([zero, sin, pad], axis=1)
    two = lambda t: jnp.concatenate([t, t], axis=1)
    return two(cos_h), two(lo_h), two(hi_h)


def _group_attention(qg, kg, vg, mask, sink):
    s = jnp.where(mask, _dot_nt(qg, kg), NEG_INF)
    m = jnp.maximum(jnp.max(s, axis=-1, keepdims=True), sink)
    p = jnp.exp(s - m)
    denom = jnp.sum(p, axis=-1, keepdims=True) + jnp.exp(sink - m)
    return _dot((p * (1.0 / denom)).astype(BF16), vg)


def _swa_prompt_kernel(q_ref, kp_ref, ko_ref, vp_ref, vo_ref, sink_ref, o_ref):
    blk = pl.program_id(1)
    W = WINDOW
    kk = jnp.concatenate([kp_ref[0], ko_ref[0]], axis=0).astype(BF16)
    vv = jnp.concatenate([vp_ref[0], vo_ref[0]], axis=0).astype(BF16)
    R = S_GROUP * W
    r = lax.broadcasted_iota(jnp.int32, (R, 2 * W), 0) % W
    j = lax.broadcasted_iota(jnp.int32, (R, 2 * W), 1)
    rel = r + W - j
    mask = (rel >= 0) & (rel <= W) & ((j >= W) | (blk > 0))
    q = q_ref[0]
    for g in range(S_KV):
        heads = range(g * S_GROUP, (g + 1) * S_GROUP)
        qg = jnp.concatenate([q[:, h * S_HD:(h + 1) * S_HD] for h in heads], axis=0)
        sink = jnp.concatenate(
            [jnp.broadcast_to(sink_ref[:, h:h + 1], (W, 1)) for h in heads], axis=0)
        og = _group_attention(qg, kk[:, g * S_HD:(g + 1) * S_HD], vv[:, g * S_HD:(g + 1) * S_HD], mask, sink)
        for i, h in enumerate(heads):
            o_ref[0, :, h * S_HD:(h + 1) * S_HD] = og[i * W:(i + 1) * W].astype(o_ref.dtype)


def _swa_prompt(q, k, v, sinks, B, T):
    W = WINDOW
    nb = T // W
    prev = lambda b, i: (b, jnp.maximum(i - 1, 0), 0)
    own = lambda b, i: (b, i, 0)
    return pl.pallas_call(
        _swa_prompt_kernel,
        grid=(B, nb),
        in_specs=[
            pl.BlockSpec((1, W, S_HQ), own),
            pl.BlockSpec((1, W, S_HKV), prev),
            pl.BlockSpec((1, W, S_HKV), own),
            pl.BlockSpec((1, W, S_HKV), prev),
            pl.BlockSpec((1, W, S_HKV), own),
            _resident((1, S_HEADS)),
        ],
        out_specs=pl.BlockSpec((1, W, S_HQ), own),
        out_shape=jax.ShapeDtypeStruct((B, T, S_HQ), BF16),
        compiler_params=_params("parallel", "parallel"),
        name="swa_prompt",
    )(q, k, k, v, v, sinks)


def _swa_sample_kernel(q_ref, kn_ref, vn_ref, kc_ref, vc_ref, sink_ref, o_ref, ko_ref, vo_ref, *, T):
    nb = kc_ref.shape[0]
    W = kc_ref.shape[1]
    R = S_GROUP * T
    t = lax.broadcasted_iota(jnp.int32, (R, W + T), 0) % T
    j = lax.broadcasted_iota(jnp.int32, (R, W + T), 1)
    rel = t + W - j
    mask = (rel >= 0) & (rel <= WINDOW)
    for b in range(nb):
        rows = slice(b * T, (b + 1) * T)
        k_all = jnp.concatenate([kc_ref[b], kn_ref[rows, :]], axis=0)
        v_all = jnp.concatenate([vc_ref[b], vn_ref[rows, :]], axis=0)
        ko_ref[b] = k_all[T:]
        vo_ref[b] = v_all[T:]
        kk = k_all.astype(BF16)
        vv = v_all.astype(BF16)
        q = q_ref[rows, :]
        for g in range(S_KV):
            heads = range(g * S_GROUP, (g + 1) * S_GROUP)
            qg = jnp.concatenate([q[:, h * S_HD:(h + 1) * S_HD] for h in heads], axis=0)
            sink = jnp.concatenate(
                [jnp.broadcast_to(sink_ref[:, h:h + 1], (T, 1)) for h in heads], axis=0)
            og = _group_attention(qg, kk[:, g * S_HD:(g + 1) * S_HD], vv[:, g * S_HD:(g + 1) * S_HD], mask, sink)
            for i, h in enumerate(heads):
                o_ref[rows, h * S_HD:(h + 1) * S_HD] = og[i * T:(i + 1) * T].astype(o_ref.dtype)


def _swa_sample(q, kn, vn, kc, vc, sinks, B, T):
    nb = SAMPLE_BLOCK
    W = kc.shape[1]
    rows = nb * T
    return pl.pallas_call(
        functools.partial(_swa_sample_kernel, T=T),
        grid=(B // nb,),
        in_specs=[
            pl.BlockSpec((rows, S_HQ), lambda i: (i, 0)),
            pl.BlockSpec((rows, S_HKV), lambda i: (i, 0)),
            pl.BlockSpec((rows, S_HKV), lambda i: (i, 0)),
            pl.BlockSpec((nb, W, S_HKV), lambda i: (i, 0, 0)),
            pl.BlockSpec((nb, W, S_HKV), lambda i: (i, 0, 0)),
            _resident((1, S_HEADS)),
        ],
        out_specs=[
            pl.BlockSpec((rows, S_HQ), lambda i: (i, 0)),
            pl.BlockSpec((nb, W, S_HKV), lambda i: (i, 0, 0)),
            pl.BlockSpec((nb, W, S_HKV), lambda i: (i, 0, 0)),
        ],
        out_shape=[
            jax.ShapeDtypeStruct((B * T, S_HQ), BF16),
            jax.ShapeDtypeStruct((B, W, S_HKV), F32),
            jax.ShapeDtypeStruct((B, W, S_HKV), F32),
        ],
        compiler_params=_params("parallel"),
        name="swa_sample",
    )(q, kn, vn, kc, vc, sinks)


def _proj_residual_kernel(a_ref, w_ref, x_ref, y_ref):
    y_ref[...] = x_ref[...] + _dot(a_ref[...], w_ref[...])


def _proj_residual(a, w, x):
    n = x.shape[0]
    tm = TOKEN_TILE
    return pl.pallas_call(
        _proj_residual_kernel,
        grid=(n // tm,),
        in_specs=[
            pl.BlockSpec((tm, a.shape[1]), lambda i: (i, 0)),
            _resident(w.shape),
            pl.BlockSpec((tm, D_MODEL), lambda i: (i, 0)),
        ],
        out_specs=pl.BlockSpec((tm, D_MODEL), lambda i: (i, 0)),
        out_shape=jax.ShapeDtypeStruct((n, D_MODEL), F32),
        compiler_params=_params("parallel"),
        name="proj_residual",
    )(a, w, x)


def _gelu(x):
    return 0.5 * x * (1.0 + lax.erf(x * (2.0 ** -0.5)))


def _convglu_body(x_ref, nw_ref, wg_ref, wu_ref, cw_ref, cb_ref, wd_ref, y_ref, act_ref, prev_rows):
    hb = _rmsnorm(x_ref[...], nw_ref[...]).astype(BF16)
    for f in range(D_FF // FF_CHUNK):
        cols = slice(f * FF_CHUNK, (f + 1) * FF_CHUNK)
        g = _dot(hb, wg_ref[:, cols])
        u = _dot(hb, wu_ref[:, cols])
        g1, g2 = prev_rows(cols, g)
        gc = cb_ref[:, cols] + cw_ref[0:1, cols] * g2 + cw_ref[1:2, cols] * g1 + cw_ref[2:3, cols] * g
        act_ref[:, cols] = (_gelu(gc) * u).astype(BF16)
    y_ref[...] = x_ref[...] + _dot(act_ref[...], wd_ref[...])


def _convglu_prompt_kernel(x_ref, nw_ref, wg_ref, wu_ref, cw_ref, cb_ref, wd_ref,
                           y_ref, g_ref, act_ref, tail_ref, *, tiles_per_seq):
    tm = x_ref.shape[0]
    row = lax.broadcasted_iota(jnp.int32, (tm, FF_CHUNK), 0)

    @pl.when(pl.program_id(0) % tiles_per_seq == 0)
    def _():
        tail_ref[...] = jnp.zeros_like(tail_ref)

    def prev_rows(cols, g):
        tail = tail_ref[:, cols]
        last, last2 = tail[SUBLANES - 1:SUBLANES, :], tail[SUBLANES - 2:SUBLANES - 1, :]
        g1 = jnp.where(row < 1, last, pltpu.roll(g, 1, 0))
        g2 = jnp.where(row < 2, jnp.where(row < 1, last2, last), pltpu.roll(g, 2, 0))
        tail_ref[:, cols] = g[tm - SUBLANES:, :]
        return g1, g2

    _convglu_body(x_ref, nw_ref, wg_ref, wu_ref, cw_ref, cb_ref, wd_ref, y_ref, act_ref, prev_rows)
    g_ref[0] = tail_ref[...]


def _convglu_sample_kernel(x_ref, nw_ref, wg_ref, wu_ref, cw_ref, cb_ref, wd_ref, e_ref,
                           y_ref, g_ref, act_ref, *, seq_len):
    tm = x_ref.shape[0]
    tpos = lax.broadcasted_iota(jnp.int32, (tm, FF_CHUNK), 0) % seq_len

    def prev_rows(cols, g):
        g_ref[:, cols] = g
        e = e_ref[:, cols]
        g1 = jnp.where(tpos < 1, pltpu.roll(e, tm - 1, 0), pltpu.roll(g, 1, 0))
        g2 = jnp.where(tpos < 2, e, pltpu.roll(g, 2, 0))
        return g1, g2

    _convglu_body(x_ref, nw_ref, wg_ref, wu_ref, cw_ref, cb_ref, wd_ref, y_ref, act_ref, prev_rows)


def _convglu_prompt(x, nw, wg, wu, cw, cb, wd, B, T):
    n = x.shape[0]
    tm = TOKEN_TILE
    assert T % tm == 0
    tiles_per_seq = T // tm

    return pl.pallas_call(
        functools.partial(_convglu_prompt_kernel, tiles_per_seq=tiles_per_seq),
        grid=(n // tm,),
        in_specs=[
            pl.BlockSpec((tm, D_MODEL), lambda i: (i, 0)),
            _resident((1, D_MODEL)),
            _resident((D_MODEL, D_FF)),
            _resident((D_MODEL, D_FF)),
            _resident((CONV_W, D_FF)),
            _resident((1, D_FF)),
            _resident((D_FF, D_MODEL)),
        ],
        out_specs=[
            pl.BlockSpec((tm, D_MODEL), lambda i: (i, 0)),
            pl.BlockSpec((1, SUBLANES, D_FF), lambda i: (i // tiles_per_seq, 0, 0)),
        ],
        out_shape=[
            jax.ShapeDtypeStruct((n, D_MODEL), F32),
            jax.ShapeDtypeStruct((B, SUBLANES, D_FF), F32),
        ],
        scratch_shapes=[pltpu.VMEM((tm, D_FF), BF16), pltpu.VMEM((SUBLANES, D_FF), F32)],
        compiler_params=_params("arbitrary"),
        name="convglu_prompt",
    )(x, nw, wg, wu, cw, cb, wd)


def _convglu_sample(x, nw, wg, wu, cw, cb, wd, e, T):
    n = x.shape[0]
    tm = TOKEN_TILE
    assert tm % T == 0 and T >= CONV_W - 1 and CONV_W == 3

    return pl.pallas_call(
        functools.partial(_convglu_sample_kernel, seq_len=T),
        grid=(n // tm,),
        in_specs=[
            pl.BlockSpec((tm, D_MODEL), lambda i: (i, 0)),
            _resident((1, D_MODEL)),
            _resident((D_MODEL, D_FF)),
            _resident((D_MODEL, D_FF)),
            _resident((CONV_W, D_FF)),
            _resident((1, D_FF)),
            _resident((D_FF, D_MODEL)),
            pl.BlockSpec((tm, D_FF), lambda i: (i, 0)),
        ],
        out_specs=[
            pl.BlockSpec((tm, D_MODEL), lambda i: (i, 0)),
            pl.BlockSpec((tm, D_FF), lambda i: (i, 0)),
        ],
        out_shape=[
            jax.ShapeDtypeStruct((n, D_MODEL), F32),
            jax.ShapeDtypeStruct((n, D_FF), F32),
        ],
        scratch_shapes=[pltpu.VMEM((tm, D_FF), BF16)],
        compiler_params=_params("parallel"),
        name="convglu_sample",
    )(x, nw, wg, wu, cw, cb, wd, e)


def kernel(x_prompt, x_sample, state_mlstm_C, state_mlstm_n, state_mlstm_m, cache_swa_k, cache_swa_v,
           state_ffn_conv, norm_mix, norm_ffn, w_mlstm_in, b_mlstm_gates, mlstm_head_norm, w_mlstm_out,
           w_swa_qkv, swa_q_norm, swa_k_norm, swa_sinks, w_swa_out, w_ffn_up, ffn_conv_w, ffn_conv_b,
           w_ffn_down):
    B, T, _ = x_prompt.shape
    Bs, Ts, _ = x_sample.shape
    W = cache_swa_k.shape[2]
    xp = x_prompt.reshape(B * T, D_MODEL)
    xs = x_sample.reshape(Bs * Ts, D_MODEL)
    n_gates = 2 * M_HEADS

    tab_p = _rope_tables(jnp.arange(T, dtype=F32))
    tab_s = tuple(jnp.tile(t, (TOKEN_TILE // Ts, 1))
                  for t in _rope_tables(PAST_LEN + jnp.arange(Ts, dtype=F32)))

    pC, pn, pm, pk, pv, pconv = [], [], [], [], [], []
    sC, sn, sm, sk, sv, sconv = [], [], [], [], [], []
    for i in range(DEPTH):
        j = i // 2
        nw = norm_mix[i][None]
        if i % 2 == 0:
            w_main = w_mlstm_in[j][:, :M_PROJ].astype(BF16)
            wg = jnp.pad(w_mlstm_in[j][:, M_PROJ:], ((0, 0), (0, LANES - n_gates)))
            bg = jnp.pad(b_mlstm_gates[j], (0, LANES - n_gates))[None]
            hw = mlstm_head_norm[j][None]
            wo = w_mlstm_out[j].astype(BF16)

            proj, gates = _mlstm_inproj(xp, nw, w_main, wg, bg, BF16)
            hs, C1, n1, m1 = _mlstm_prompt(proj.reshape(B, T, M_PROJ), gates.reshape(B, T, LANES), B, T)
            xp = _mlstm_out(hs.reshape(B * T, M_HV), proj, hw, wo, xp)
            pC.append(C1)
            pn.append(n1[:, :, 0])
            pm.append(m1[:, :, 0, 0])

            proj, gates = _mlstm_inproj(xs, nw, w_main, wg, bg, F32)
            m0 = jnp.broadcast_to(state_mlstm_m[j][:, :, None, None], (Bs, M_HEADS, 1, LANES))
            hs, C2, n2, m2 = _mlstm_sample(proj, gates, state_mlstm_C[j], state_mlstm_n[j][:, :, None, :],
                                           m0, Bs, Ts)
            xs = _mlstm_out(hs, proj, hw, wo, xs)
            sC.append(C2)
            sn.append(n2[:, :, 0])
            sm.append(m2[:, :, 0, 0])
        else:
            w = w_swa_qkv[j].astype(BF16)
            qw = jnp.tile(swa_q_norm[j], LANES // S_HD)[None]
            kw = jnp.tile(swa_k_norm[j], LANES // S_HD)[None]
            sinks = swa_sinks[j][None]
            wo = w_swa_out[j].astype(BF16)

            q, k, v = _swa_qkv(xp, nw, w, qw, kw, *tab_p)
            o = _swa_prompt(q.reshape(B, T, S_HQ), k.reshape(B, T, S_HKV), v.reshape(B, T, S_HKV), sinks, B, T)
            xp = _proj_residual(o.reshape(B * T, S_HQ), wo, xp)
            pk.append(k.reshape(B, T, S_KV, S_HD)[:, T - WINDOW:])
            pv.append(v.reshape(B, T, S_KV, S_HD)[:, T - WINDOW:])

            q, k, v = _swa_qkv(xs, nw, w, qw, kw, *tab_s)
            o, k2, v2 = _swa_sample(q, k, v, cache_swa_k[j].reshape(Bs, W, S_HKV),
                                    cache_swa_v[j].reshape(Bs, W, S_HKV), sinks, Bs, Ts)
            xs = _proj_residual(o, wo, xs)
            sk.append(k2.reshape(Bs, W, S_KV, S_HD))
            sv.append(v2.reshape(Bs, W, S_KV, S_HD))

        nwf = norm_ffn[i][None]
        wg = w_ffn_up[i][:, :D_FF].astype(BF16)
        wu = w_ffn_up[i][:, D_FF:].astype(BF16)
        cw = ffn_conv_w[i]
        cb = ffn_conv_b[i][None]
        wd = w_ffn_down[i].astype(BF16)

        xp, gtail = _convglu_prompt(xp, nwf, wg, wu, cw, cb, wd, B, T)
        pconv.append(gtail[:, SUBLANES - (CONV_W - 1):])

        st = state_ffn_conv[i]
        e = jnp.pad(st, ((0, 0), (0, Ts - (CONV_W - 1)), (0, 0))).reshape(Bs * Ts, D_FF)
        xs, g = _convglu_sample(xs, nwf, wg, wu, cw, cb, wd, e, Ts)
        sconv.append(g.reshape(Bs, Ts, D_FF)[:, Ts - (CONV_W - 1):])

    return (xp.reshape(B, T, D_MODEL), xs.reshape(Bs, Ts, D_MODEL),
            jnp.stack(pC), jnp.stack(pn), jnp.stack(pm), jnp.stack(pk), jnp.stack(pv), jnp.stack(pconv),
            jnp.stack(sC), jnp.stack(sn), jnp.stack(sm), jnp.stack(sk), jnp.stack(sv), jnp.stack(sconv))
```

```python
import functools
import math

import jax
import jax.numpy as jnp
from jax import lax
from jax.experimental import pallas as pl
from jax.experimental.pallas import tpu as pltpu

F32 = jnp.float32
BF16 = jnp.bfloat16

D_MODEL = 1024
DEPTH = 4
PAST_LEN = 8192
M_HEADS = 4
M_DV = D_MODEL // M_HEADS
M_DK = M_DV // 2
M_HK = M_HEADS * M_DK
M_HV = M_HEADS * M_DV
M_PROJ = 2 * M_HK + 2 * M_HV
S_HD = 64
S_HEADS = D_MODEL // S_HD
S_KV = 4
S_GROUP = S_HEADS // S_KV
S_HQ = S_HEADS * S_HD
S_HKV = S_KV * S_HD
WINDOW = 128
ROT_DIM = S_HD // 4
ROPE_THETA = 500000.0
D_FF = 2816
CONV_W = 3
EPS = 1e-6

LANES = 128
SUBLANES = 8
VMEM_LIMIT = 56 * 1024 * 1024
TOKEN_TILE = 512
FF_CHUNK = 256
PROJ_CHUNK = 512
MLSTM_CHUNK = 256
SAMPLE_BLOCK = 8
SAMPLE_UNROLL = 2
SCORE_LOOKAHEAD = 16
NEG_INF = float("-inf")


def _params(*sem):
    return pltpu.CompilerParams(dimension_semantics=sem, vmem_limit_bytes=VMEM_LIMIT)


def _resident(shape):
    zeros = (0,) * len(shape)
    return pl.BlockSpec(shape, lambda *_: zeros, pipeline_mode=pl.Buffered(1))


def _rmsnorm(x, w):
    ms = jnp.mean(x * x, axis=-1, keepdims=True)
    return x * lax.rsqrt(ms + EPS) * w


def _dot(a, b):
    return jnp.dot(a, b, preferred_element_type=F32)


def _dot_nt(a, b):
    return lax.dot_general(a, b, (((1,), (1,)), ((), ())), preferred_element_type=F32)


def _dot_tn(a, b):
    return lax.dot_general(a, b, (((0,), (0,)), ((), ())), preferred_element_type=F32)


def _mlstm_inproj_kernel(x_ref, nw_ref, w_ref, bg_ref, proj_ref, gate_ref):
    hb = _rmsnorm(x_ref[...], nw_ref[...]).astype(BF16)
    for j in range(M_PROJ // PROJ_CHUNK):
        cols = slice(j * PROJ_CHUNK, (j + 1) * PROJ_CHUNK)
        r = _dot(hb, w_ref[:, cols])
        if j * PROJ_CHUNK == M_HK:
            r = r * (M_DK ** -0.5)
        proj_ref[:, cols] = r.astype(proj_ref.dtype)
    g = _dot(hb, w_ref[:, M_PROJ:]) + bg_ref[...]
    logf = -(jnp.maximum(-g, 0.0) + jnp.log1p(jnp.exp(-jnp.abs(g))))
    lane = lax.broadcasted_iota(jnp.int32, g.shape, 1)
    gate_ref[...] = jnp.where(lane < M_HEADS, g, logf)


def _mlstm_inproj(x, nw, w, bg, proj_dtype):
    n = x.shape[0]
    tm = TOKEN_TILE
    assert n % tm == 0 and M_HK == PROJ_CHUNK
    return pl.pallas_call(
        _mlstm_inproj_kernel,
        grid=(n // tm,),
        in_specs=[
            pl.BlockSpec((tm, D_MODEL), lambda i: (i, 0)),
            _resident((1, D_MODEL)),
            _resident((D_MODEL, M_PROJ + LANES)),
            _resident((1, LANES)),
        ],
        out_specs=[
            pl.BlockSpec((tm, M_PROJ), lambda i: (i, 0)),
            pl.BlockSpec((tm, LANES), lambda i: (i, 0)),
        ],
        out_shape=[
            jax.ShapeDtypeStruct((n, M_PROJ), proj_dtype),
            jax.ShapeDtypeStruct((n, LANES), F32),
        ],
        compiler_params=_params("parallel"),
        name="mlstm_inproj",
    )(x, nw, w, bg)


def _mlstm_chunks(units):
    L = units[0][0].shape[0]
    t_idx = lax.broadcasted_iota(jnp.int32, (L, L), 0)
    s_idx = lax.broadcasted_iota(jnp.int32, (L, L), 1)
    causal = s_idx <= t_idx
    eye = s_idx == t_idx

    def to_row(col):
        return jnp.sum(jnp.where(eye, col, 0.0), axis=0, keepdims=True)

    first = [(_dot_nt(q, k), _dot(q, C.astype(BF16))) for q, k, v, _, _, C, _, _ in units]
    mid = []
    for (q, k, v, i_col, f_col, C, n, m), (qk, qC) in zip(units, first):
        f_row = to_row(f_col)
        i_row = to_row(i_col)
        b_col = jnp.sum(jnp.where(causal, f_row, 0.0), axis=1, keepdims=True)
        b_row = to_row(b_col)
        d = jnp.where(causal, b_col - b_row + i_row, NEG_INF)
        m_inter = b_col + m
        m_t = jnp.maximum(m_inter, jnp.max(d, axis=1, keepdims=True))
        w_intra = jnp.exp(d - m_t) * qk
        w_inter = jnp.exp(m_inter - m_t)
        qn = jnp.sum(q.astype(F32) * n, axis=1, keepdims=True)
        nq = w_inter * qn + jnp.sum(w_intra, axis=1, keepdims=True)
        inv_den = 1.0 / jnp.maximum(jnp.abs(nq), jnp.exp(-m_t))
        m_new = m_t[L - 1:L, :]
        b_last = b_col[L - 1:L, :]
        w_state = jnp.exp(b_last - b_col + i_col - m_new)
        decay = jnp.exp(b_last + m - m_new)
        kw = k.astype(F32) * w_state
        n_new = decay * n + jnp.sum(kw, axis=0, keepdims=True)
        mid.append((w_intra.astype(BF16), kw.astype(BF16), w_inter, inv_den, decay, n_new, m_new))
    second = [(_dot(w_intra, u[2]), _dot_tn(kw, u[2])) for u, (w_intra, kw, *_) in zip(units, mid)]
    out = []
    for u, (_, qC), (_, _, w_inter, inv_den, decay, n_new, m_new), (wv, kv) in zip(units, first, mid, second):
        h = (w_inter * qC + wv) * inv_den
        out.append((h, decay * u[5] + kv, n_new, m_new))
    return out


def _mlstm_prompt_kernel(q_ref, k_ref, v_ref, g_ref, hs_ref, C_ref, n_ref, m_ref, *, chunk):
    T = q_ref.shape[1]
    C_ref[...] = jnp.zeros_like(C_ref)
    n_ref[...] = jnp.zeros_like(n_ref)
    m_ref[...] = jnp.zeros_like(m_ref)

    def body(c, carry):
        rows = pl.ds(pl.multiple_of(c * chunk, chunk), chunk)
        g = g_ref[0, rows, :]
        units = []
        for hd in range(M_HEADS):
            units.append((q_ref[0, rows, hd * M_DK:(hd + 1) * M_DK],
                          k_ref[0, rows, hd * M_DK:(hd + 1) * M_DK],
                          v_ref[0, rows, hd * M_DV:(hd + 1) * M_DV],
                          g[:, hd:hd + 1], g[:, M_HEADS + hd:M_HEADS + hd + 1],
                          C_ref[0, hd], n_ref[0, hd], m_ref[0, hd, :, 0:1]))
        for hd, (h, C_new, n_new, m_new) in enumerate(_mlstm_chunks(units)):
            hs_ref[0, rows, hd * M_DV:(hd + 1) * M_DV] = h.astype(hs_ref.dtype)
            C_ref[0, hd] = C_new
            n_ref[0, hd] = n_new
            m_ref[0, hd] = jnp.broadcast_to(m_new, (1, LANES))
        return carry

    lax.fori_loop(0, T // chunk, body, 0)


def _mlstm_prompt(proj, gates, B, T):
    chunk = MLSTM_CHUNK
    assert T % chunk == 0
    return pl.pallas_call(
        functools.partial(_mlstm_prompt_kernel, chunk=chunk),
        grid=(B,),
        in_specs=[
            pl.BlockSpec((1, T, M_HK), lambda b: (b, 0, 0)),
            pl.BlockSpec((1, T, M_HK), lambda b: (b, 0, 1)),
            pl.BlockSpec((1, T, M_HV), lambda b: (b, 0, 2 * M_HK // M_HV)),
            pl.BlockSpec((1, T, LANES), lambda b: (b, 0, 0)),
        ],
        out_specs=[
            pl.BlockSpec((1, T, M_HV), lambda b: (b, 0, 0)),
            pl.BlockSpec((1, M_HEADS, M_DK, M_DV), lambda b: (b, 0, 0, 0)),
            pl.BlockSpec((1, M_HEADS, 1, M_DK), lambda b: (b, 0, 0, 0)),
            pl.BlockSpec((1, M_HEADS, 1, LANES), lambda b: (b, 0, 0, 0)),
        ],
        out_shape=[
            jax.ShapeDtypeStruct((B, T, M_HV), BF16),
            jax.ShapeDtypeStruct((B, M_HEADS, M_DK, M_DV), F32),
            jax.ShapeDtypeStruct((B, M_HEADS, 1, M_DK), F32),
            jax.ShapeDtypeStruct((B, M_HEADS, 1, LANES), F32),
        ],
        compiler_params=_params("parallel"),
        name="mlstm_prompt",
    )(proj, proj, proj, gates)


def _mlstm_sample_kernel(q_ref, k_ref, v_ref, g_ref, C0_ref, n0_ref, m0_ref,
                         hs_ref, C_ref, n_ref, m_ref, *, T):
    nb = C0_ref.shape[0]

    def body(i, carry):
        units, where = [], []
        for s in range(SAMPLE_UNROLL):
            b = i * SAMPLE_UNROLL + s
            rows = pl.ds(pl.multiple_of(b * T, T), T)
            g = g_ref[rows, :]
            for hd in range(M_HEADS):
                units.append((q_ref[rows, hd * M_DK:(hd + 1) * M_DK].astype(BF16),
                              k_ref[rows, hd * M_DK:(hd + 1) * M_DK].astype(BF16),
                              v_ref[rows, hd * M_DV:(hd + 1) * M_DV].astype(BF16),
                              g[:, hd:hd + 1], g[:, M_HEADS + hd:M_HEADS + hd + 1],
                              C0_ref[b, hd], n0_ref[b, hd], m0_ref[b, hd, :, 0:1]))
                where.append((b, rows, hd))
        for (b, rows, hd), (h, C_new, n_new, m_new) in zip(where, _mlstm_chunks(units)):
            hs_ref[rows, hd * M_DV:(hd + 1) * M_DV] = h.astype(hs_ref.dtype)
            C_ref[b, hd] = C_new
            n_ref[b, hd] = n_new
            m_ref[b, hd] = jnp.broadcast_to(m_new, (1, LANES))
        return carry

    lax.fori_loop(0, nb // SAMPLE_UNROLL, body, 0)


def _mlstm_sample(proj, gates, C0, n0, m0, B, T):
    nb = SAMPLE_BLOCK
    assert B % nb == 0 and T % SUBLANES == 0
    rows = nb * T
    return pl.pallas_call(
        functools.partial(_mlstm_sample_kernel, T=T),
        grid=(B // nb,),
        in_specs=[
            pl.BlockSpec((rows, M_HK), lambda i: (i, 0)),
            pl.BlockSpec((rows, M_HK), lambda i: (i, 1)),
            pl.BlockSpec((rows, M_HV), lambda i: (i, 2 * M_HK // M_HV)),
            pl.BlockSpec((rows, LANES), lambda i: (i, 0)),
            pl.BlockSpec((nb, M_HEADS, M_DK, M_DV), lambda i: (i, 0, 0, 0)),
            pl.BlockSpec((nb, M_HEADS, 1, M_DK), lambda i: (i, 0, 0, 0)),
            pl.BlockSpec((nb, M_HEADS, 1, LANES), lambda i: (i, 0, 0, 0)),
        ],
        out_specs=[
            pl.BlockSpec((rows, M_HV), lambda i: (i, 0)),
            pl.BlockSpec((nb, M_HEADS, M_DK, M_DV), lambda i: (i, 0, 0, 0)),
            pl.BlockSpec((nb, M_HEADS, 1, M_DK), lambda i: (i, 0, 0, 0)),
            pl.BlockSpec((nb, M_HEADS, 1, LANES), lambda i: (i, 0, 0, 0)),
        ],
        out_shape=[
            jax.ShapeDtypeStruct((B * T, M_HV), BF16),
            jax.ShapeDtypeStruct((B, M_HEADS, M_DK, M_DV), F32),
            jax.ShapeDtypeStruct((B, M_HEADS, 1, M_DK), F32),
            jax.ShapeDtypeStruct((B, M_HEADS, 1, LANES), F32),
        ],
        compiler_params=_params("parallel"),
        name="mlstm_sample",
    )(proj, proj, proj, gates, C0, n0, m0)


def _mlstm_out_kernel(hs_ref, o_ref, hw_ref, w_ref, x_ref, y_ref):
    hs = hs_ref[...].astype(F32)
    parts = []
    for hd in range(M_HEADS):
        seg = hs[:, hd * M_DV:(hd + 1) * M_DV]
        ms = jnp.mean(seg * seg, axis=-1, keepdims=True)
        parts.append(seg * lax.rsqrt(ms + EPS))
    hn = jnp.concatenate(parts, axis=1) * hw_ref[...]
    a = hn * jax.nn.sigmoid(o_ref[...].astype(F32))
    y_ref[...] = x_ref[...] + _dot(a.astype(BF16), w_ref[...])


def _mlstm_out(hs, proj, hw, w, x):
    n = x.shape[0]
    tm = TOKEN_TILE
    o_block = (2 * M_HK + M_HV) // M_HV
    return pl.pallas_call(
        _mlstm_out_kernel,
        grid=(n // tm,),
        in_specs=[
            pl.BlockSpec((tm, M_HV), lambda i: (i, 0)),
            pl.BlockSpec((tm, M_HV), lambda i: (i, o_block)),
            _resident((1, M_HV)),
            _resident((M_HV, D_MODEL)),
            pl.BlockSpec((tm, D_MODEL), lambda i: (i, 0)),
        ],
        out_specs=pl.BlockSpec((tm, D_MODEL), lambda i: (i, 0)),
        out_shape=jax.ShapeDtypeStruct((n, D_MODEL), F32),
        compiler_params=_params("parallel"),
        name="mlstm_out",
    )(hs, proj, hw, w, x)


def _head_norm_rope(y, w, cos, s_lo, s_hi):
    lane = lax.broadcasted_iota(jnp.int32, y.shape, 1)
    first = lane < S_HD
    sq = y * y
    lo = jnp.sum(jnp.where(first, sq, 0.0), axis=-1, keepdims=True)
    hi = jnp.sum(jnp.where(first, 0.0, sq), axis=-1, keepdims=True)
    ms = jnp.where(first, lo, hi) * (1.0 / S_HD)
    z = y * lax.rsqrt(ms + EPS) * w
    half = ROT_DIM // 2
    return z * cos + pltpu.roll(z, LANES - half, 1) * s_lo + pltpu.roll(z, half, 1) * s_hi


def _swa_qkv_kernel(x_ref, nw_ref, w_ref, wvt_ref, qw_ref, kw_ref, cos_ref, slo_ref, shi_ref,
                    q_ref, k_ref, v_ref, vt_ref):
    hb = _rmsnorm(x_ref[...], nw_ref[...]).astype(BF16)
    cos, s_lo, s_hi = cos_ref[...], slo_ref[...], shi_ref[...]
    for j in range(S_HQ // LANES):
        cols = slice(j * LANES, (j + 1) * LANES)
        y = _head_norm_rope(_dot(hb, w_ref[:, cols]), qw_ref[...], cos, s_lo, s_hi)
        q_ref[:, cols] = (y * (S_HD ** -0.5)).astype(q_ref.dtype)
    for j in range(S_HKV // LANES):
        cols = slice(j * LANES, (j + 1) * LANES)
        wcols = slice(S_HQ + j * LANES, S_HQ + (j + 1) * LANES)
        k_ref[:, cols] = _head_norm_rope(_dot(hb, w_ref[:, wcols]), kw_ref[...], cos, s_lo, s_hi)
    v_ref[...] = _dot(hb, w_ref[:, S_HQ + S_HKV:])
    vt_ref[...] = _dot_nt(wvt_ref[...], hb).astype(vt_ref.dtype)


def _swa_qkv(x, nw, w, wvt, qw, kw, cos, s_lo, s_hi):
    n = x.shape[0]
    tm = TOKEN_TILE
    tab_blocks = cos.shape[0] // tm
    tab = pl.BlockSpec((tm, LANES), lambda i: (i % tab_blocks, 0))
    return pl.pallas_call(
        _swa_qkv_kernel,
        grid=(n // tm,),
        in_specs=[
            pl.BlockSpec((tm, D_MODEL), lambda i: (i, 0)),
            _resident((1, D_MODEL)),
            _resident((D_MODEL, S_HQ + 2 * S_HKV)),
            _resident((S_HKV, D_MODEL)),
            _resident((1, LANES)),
            _resident((1, LANES)),
            tab, tab, tab,
        ],
        out_specs=[
            pl.BlockSpec((tm, S_HQ), lambda i: (i, 0)),
            pl.BlockSpec((tm, S_HKV), lambda i: (i, 0)),
            pl.BlockSpec((tm, S_HKV), lambda i: (i, 0)),
            pl.BlockSpec((S_HKV, tm), lambda i: (0, i)),
        ],
        out_shape=[
            jax.ShapeDtypeStruct((n, S_HQ), BF16),
            jax.ShapeDtypeStruct((n, S_HKV), F32),
            jax.ShapeDtypeStruct((n, S_HKV), F32),
            jax.ShapeDtypeStruct((S_HKV, n), BF16),
        ],
        compiler_params=_params("parallel"),
        name="swa_qkv",
    )(x, nw, w, wvt, qw, kw, cos, s_lo, s_hi)


def _rope_tables(pos):
    half = ROT_DIM // 2
    inv = ROPE_THETA ** (-jnp.arange(half, dtype=F32) * 2.0 / ROT_DIM)
    ang = pos[:, None] * inv[None, :]
    cos, sin = jnp.cos(ang), jnp.sin(ang)
    n = pos.shape[0]
    pad = jnp.zeros((n, S_HD - ROT_DIM), F32)
    zero = jnp.zeros((n, half), F32)
    cos_h = jnp.concatenate([cos, cos, pad + 1.0], axis=1)
    lo_h = jnp.concatenate([-sin, zero, pad], axis=1)
    hi_h = jnp.concatenate([zero, sin, pad], axis=1)
    two = lambda t: jnp.concatenate([t, t], axis=1)
    return two(cos_h), two(lo_h), two(hi_h)


def _softmax_over_keys(s, mask, sink):
    s = jnp.where(mask, s, NEG_INF)
    m = jnp.maximum(jnp.max(s, axis=0, keepdims=True), sink)
    p = jnp.exp(s - m)
    denom = jnp.sum(p, axis=0, keepdims=True) + jnp.exp(sink - m)
    return p * (1.0 / denom)


def _swa_prompt_kernel(sink_ref, q_ref, kp_ref, ko_ref, vtp_ref, vto_ref, o_ref):
    blk = pl.program_id(1)
    W = WINDOW
    kf = jnp.concatenate([kp_ref[0], ko_ref[0]], axis=0)
    vt = jnp.concatenate([vtp_ref[...], vto_ref[...]], axis=1)
    j = lax.broadcasted_iota(jnp.int32, (2 * W, W), 0)
    rel = j - lax.broadcasted_iota(jnp.int32, (2 * W, W), 1)
    mask = (rel >= 0) & (rel <= W) & ((j >= W) | (blk > 0))
    low = lax.broadcasted_iota(jnp.int32, (2 * W, LANES), 1) < S_HD
    q = q_ref[0]
    halves = LANES // S_HD
    k_padded = {}
    for c in range(S_HKV // LANES):
        kc = kf[:, c * LANES:(c + 1) * LANES]
        kr = pltpu.roll(kc, S_HD, 1)
        for half in range(halves):
            k_lo = jnp.where(low, kc if half == 0 else kr, 0.0).astype(BF16)
            k_hi = jnp.where(low, 0.0, kr if half == 0 else kc).astype(BF16)
            k_padded[c * halves + half] = (k_lo, k_hi)

    def scores(h):
        pair = h // halves
        return _dot_nt(k_padded[h // S_GROUP][h % halves], q[:, pair * LANES:(pair + 1) * LANES])

    pending = [scores(h) for h in range(SCORE_LOOKAHEAD)]
    outs = []
    for h in range(S_HEADS):
        g = h // S_GROUP
        sink = sink_ref[h]
        s = jnp.where(mask, pending.pop(0), NEG_INF)
        m = jnp.maximum(jnp.max(s, axis=0, keepdims=True), sink)
        p = jnp.exp(s - m)
        denom = jnp.sum(p, axis=0, keepdims=True) + jnp.exp(sink - m)
        if h + SCORE_LOOKAHEAD < S_HEADS:
            pending.append(scores(h + SCORE_LOOKAHEAD))
        outs.append(_dot(vt[g * S_HD:(g + 1) * S_HD, :], p.astype(BF16)) * (1.0 / denom))
        if len(outs) == halves:
            pair = h // halves
            o_pair = jnp.concatenate(outs, axis=0).T
            o_ref[0, :, pair * LANES:(pair + 1) * LANES] = o_pair.astype(o_ref.dtype)
            outs = []


def _swa_prompt(q, k, vt, sinks, B, T):
    W = WINDOW
    nb = T // W
    prev = lambda b, i: (b, jnp.maximum(i - 1, 0), 0)
    own = lambda b, i: (b, i, 0)
    return pl.pallas_call(
        _swa_prompt_kernel,
        grid=(B, nb),
        in_specs=[
            pl.BlockSpec(memory_space=pltpu.SMEM),
            pl.BlockSpec((1, W, S_HQ), own),
            pl.BlockSpec((1, W, S_HKV), prev),
            pl.BlockSpec((1, W, S_HKV), own),
            pl.BlockSpec((S_HKV, W), lambda b, i: (0, b * nb + jnp.maximum(i - 1, 0))),
            pl.BlockSpec((S_HKV, W), lambda b, i: (0, b * nb + i)),
        ],
        out_specs=pl.BlockSpec((1, W, S_HQ), own),
        out_shape=jax.ShapeDtypeStruct((B, T, S_HQ), BF16),
        compiler_params=_params("parallel", "parallel"),
        name="swa_prompt",
    )(sinks, q, k, k, vt, vt)


SAMPLE_KEY_PAD = 16


def _swa_sample_kernel(q_ref, kn_ref, vn_ref, kc_ref, vc_ref, sink_ref, o_ref, ko_ref, vo_ref, *, T):
    nb, W = kc_ref.shape[0], kc_ref.shape[1]
    J = W + SAMPLE_KEY_PAD
    j = lax.broadcasted_iota(jnp.int32, (J, LANES), 0)
    t = lax.rem(lax.broadcasted_iota(jnp.int32, (J, LANES), 1), T)
    rel = t + W - j
    mask = (rel >= 0) & (rel <= WINDOW)
    low = lax.broadcasted_iota(jnp.int32, (T, LANES), 1) < S_HD
    zero = jnp.zeros((T, LANES), F32)
    kpad = jnp.zeros((SAMPLE_KEY_PAD - T, S_HKV), F32)
    sink = sink_ref[...]
    halves = LANES // S_HD
    for b in range(nb):
        rows = slice(b * T, (b + 1) * T)
        kn, vn, kc, vc = kn_ref[rows, :], vn_ref[rows, :], kc_ref[b], vc_ref[b]
        ko_ref[b, :W - T] = kc[T:]
        ko_ref[b, W - T:] = kn
        vo_ref[b, :W - T] = vc[T:]
        vo_ref[b, W - T:] = vn
        kall = jnp.concatenate([kc, kn, kpad], axis=0).astype(BF16)
        vall = jnp.concatenate([vc, vn, kpad], axis=0).astype(BF16)
        qf = q_ref[rows, :].astype(F32)
        blocks = []
        for h in range(S_HEADS):
            g = h // S_GROUP
            src = qf[:, (h // halves) * LANES:(h // halves + 1) * LANES]
            if h % halves != g % halves:
                src = pltpu.roll(src, S_HD, 1)
            src = jnp.where(low if g % halves == 0 else ~low, src, 0.0)
            blocks.append(jnp.concatenate(
                [src if c == g // halves else zero for c in range(S_HKV // LANES)], axis=1))
        qblk = jnp.concatenate(blocks, axis=0).astype(BF16)
        pn = _softmax_over_keys(_dot_nt(kall, qblk), mask, sink)
        o_full = _dot_tn(pn.astype(BF16), vall)
        for pair in range(S_HEADS // halves):
            pieces = []
            for hh in range(halves):
                h = pair * halves + hh
                g = h // S_GROUP
                piece = o_full[h * T:(h + 1) * T, (g // halves) * LANES:(g // halves + 1) * LANES]
                if g % halves != hh:
                    piece = pltpu.roll(piece, S_HD, 1)
                pieces.append(piece)
            o_ref[rows, pair * LANES:(pair + 1) * LANES] = jnp.where(low, pieces[0], pieces[1]).astype(o_ref.dtype)


def _swa_sample(q, kn, vn, kc, vc, sinks, B, T):
    nb = SAMPLE_BLOCK
    W = kc.shape[1]
    rows = nb * T
    assert S_HEADS * T == LANES and LANES // S_HD == 2 and T <= SAMPLE_KEY_PAD
    return pl.pallas_call(
        functools.partial(_swa_sample_kernel, T=T),
        grid=(B // nb,),
        in_specs=[
            pl.BlockSpec((rows, S_HQ), lambda i: (i, 0)),
            pl.BlockSpec((rows, S_HKV), lambda i: (i, 0)),
            pl.BlockSpec((rows, S_HKV), lambda i: (i, 0)),
            pl.BlockSpec((nb, W, S_HKV), lambda i: (i, 0, 0)),
            pl.BlockSpec((nb, W, S_HKV), lambda i: (i, 0, 0)),
            _resident((1, LANES)),
        ],
        out_specs=[
            pl.BlockSpec((rows, S_HQ), lambda i: (i, 0)),
            pl.BlockSpec((nb, W, S_HKV), lambda i: (i, 0, 0)),
            pl.BlockSpec((nb, W, S_HKV), lambda i: (i, 0, 0)),
        ],
        out_shape=[
            jax.ShapeDtypeStruct((B * T, S_HQ), BF16),
            jax.ShapeDtypeStruct((B, W, S_HKV), F32),
            jax.ShapeDtypeStruct((B, W, S_HKV), F32),
        ],
        compiler_params=_params("parallel"),
        name="swa_sample",
    )(q, kn, vn, kc, vc, sinks)


def _proj_residual_kernel(a_ref, w_ref, x_ref, y_ref):
    y_ref[...] = x_ref[...] + _dot(a_ref[...], w_ref[...])


def _proj_residual(a, w, x):
    n = x.shape[0]
    tm = TOKEN_TILE
    return pl.pallas_call(
        _proj_residual_kernel,
        grid=(n // tm,),
        in_specs=[
            pl.BlockSpec((tm, a.shape[1]), lambda i: (i, 0)),
            _resident(w.shape),
            pl.BlockSpec((tm, D_MODEL), lambda i: (i, 0)),
        ],
        out_specs=pl.BlockSpec((tm, D_MODEL), lambda i: (i, 0)),
        out_shape=jax.ShapeDtypeStruct((n, D_MODEL), F32),
        compiler_params=_params("parallel"),
        name="proj_residual",
    )(a, w, x)


def _gelu(x):
    return 0.5 * x * (1.0 + lax.erf(x * (2.0 ** -0.5)))


def _convglu_body(x_ref, nw_ref, wg_ref, wu_ref, cw_ref, cb_ref, wd_ref, y_ref, act_ref, prev_rows):
    hb = _rmsnorm(x_ref[...], nw_ref[...]).astype(BF16)
    for f in range(D_FF // FF_CHUNK):
        cols = slice(f * FF_CHUNK, (f + 1) * FF_CHUNK)
        g = _dot(hb, wg_ref[:, cols])
        u = _dot(hb, wu_ref[:, cols])
        g1, g2 = prev_rows(cols, g)
        gc = cb_ref[:, cols] + cw_ref[0:1, cols] * g2 + cw_ref[1:2, cols] * g1 + cw_ref[2:3, cols] * g
        act_ref[:, cols] = (_gelu(gc) * u).astype(BF16)
    y_ref[...] = x_ref[...] + _dot(act_ref[...], wd_ref[...])


def _convglu_prompt_kernel(x_ref, nw_ref, wg_ref, wu_ref, cw_ref, cb_ref, wd_ref,
                           y_ref, g_ref, act_ref, tail_ref, *, tiles_per_seq):
    tm = x_ref.shape[0]
    row = lax.broadcasted_iota(jnp.int32, (tm, FF_CHUNK), 0)

    @pl.when(pl.program_id(0) % tiles_per_seq == 0)
    def _():
        tail_ref[...] = jnp.zeros_like(tail_ref)

    def prev_rows(cols, g):
        tail = tail_ref[:, cols]
        last, last2 = tail[SUBLANES - 1:SUBLANES, :], tail[SUBLANES - 2:SUBLANES - 1, :]
        g1 = jnp.where(row < 1, last, pltpu.roll(g, 1, 0))
        g2 = jnp.where(row < 2, jnp.where(row < 1, last2, last), pltpu.roll(g, 2, 0))
        tail_ref[:, cols] = g[tm - SUBLANES:, :]
        return g1, g2

    _convglu_body(x_ref, nw_ref, wg_ref, wu_ref, cw_ref, cb_ref, wd_ref, y_ref, act_ref, prev_rows)
    g_ref[0] = tail_ref[...]


def _convglu_sample_kernel(x_ref, nw_ref, wg_ref, wu_ref, cw_ref, cb_ref, wd_ref, e_ref,
                           y_ref, g_ref, act_ref, *, seq_len):
    tm = x_ref.shape[0]
    tpos = lax.broadcasted_iota(jnp.int32, (tm, FF_CHUNK), 0) % seq_len

    def prev_rows(cols, g):
        g_ref[:, cols] = g
        e = e_ref[:, cols]
        g1 = jnp.where(tpos < 1, pltpu.roll(e, tm - 1, 0), pltpu.roll(g, 1, 0))
        g2 = jnp.where(tpos < 2, e, pltpu.roll(g, 2, 0))
        return g1, g2

    _convglu_body(x_ref, nw_ref, wg_ref, wu_ref, cw_ref, cb_ref, wd_ref, y_ref, act_ref, prev_rows)


def _convglu_prompt(x, nw, wg, wu, cw, cb, wd, B, T):
    n = x.shape[0]
    tm = TOKEN_TILE
    assert T % tm == 0
    tiles_per_seq = T // tm

    return pl.pallas_call(
        functools.partial(_convglu_prompt_kernel, tiles_per_seq=tiles_per_seq),
        grid=(n // tm,),
        in_specs=[
            pl.BlockSpec((tm, D_MODEL), lambda i: (i, 0)),
            _resident((1, D_MODEL)),
            _resident((D_MODEL, D_FF)),
            _resident((D_MODEL, D_FF)),
            _resident((CONV_W, D_FF)),
            _resident((1, D_FF)),
            _resident((D_FF, D_MODEL)),
        ],
        out_specs=[
            pl.BlockSpec((tm, D_MODEL), lambda i: (i, 0)),
            pl.BlockSpec((1, SUBLANES, D_FF), lambda i: (i // tiles_per_seq, 0, 0)),
        ],
        out_shape=[
            jax.ShapeDtypeStruct((n, D_MODEL), F32),
            jax.ShapeDtypeStruct((B, SUBLANES, D_FF), F32),
        ],
        scratch_shapes=[pltpu.VMEM((tm, D_FF), BF16), pltpu.VMEM((SUBLANES, D_FF), F32)],
        compiler_params=_params("arbitrary"),
        name="convglu_prompt",
    )(x, nw, wg, wu, cw, cb, wd)


def _convglu_sample(x, nw, wg, wu, cw, cb, wd, e, T):
    n = x.shape[0]
    tm = TOKEN_TILE
    assert tm % T == 0 and T >= CONV_W - 1 and CONV_W == 3

    return pl.pallas_call(
        functools.partial(_convglu_sample_kernel, seq_len=T),
        grid=(n // tm,),
        in_specs=[
            pl.BlockSpec((tm, D_MODEL), lambda i: (i, 0)),
            _resident((1, D_MODEL)),
            _resident((D_MODEL, D_FF)),
            _resident((D_MODEL, D_FF)),
            _resident((CONV_W, D_FF)),
            _resident((1, D_FF)),
            _resident((D_FF, D_MODEL)),
            pl.BlockSpec((tm, D_FF), lambda i: (i, 0)),
        ],
        out_specs=[
            pl.BlockSpec((tm, D_MODEL), lambda i: (i, 0)),
            pl.BlockSpec((tm, D_FF), lambda i: (i, 0)),
        ],
        out_shape=[
            jax.ShapeDtypeStruct((n, D_MODEL), F32),
            jax.ShapeDtypeStruct((n, D_FF), F32),
        ],
        scratch_shapes=[pltpu.VMEM((tm, D_FF), BF16)],
        compiler_params=_params("parallel"),
        name="convglu_sample",
    )(x, nw, wg, wu, cw, cb, wd, e)


def kernel(x_prompt, x_sample, state_mlstm_C, state_mlstm_n, state_mlstm_m, cache_swa_k, cache_swa_v,
           state_ffn_conv, norm_mix, norm_ffn, w_mlstm_in, b_mlstm_gates, mlstm_head_norm, w_mlstm_out,
           w_swa_qkv, swa_q_norm, swa_k_norm, swa_sinks, w_swa_out, w_ffn_up, ffn_conv_w, ffn_conv_b,
           w_ffn_down):
    B, T, _ = x_prompt.shape
    Bs, Ts, _ = x_sample.shape
    W = cache_swa_k.shape[2]
    xp = x_prompt.reshape(B * T, D_MODEL)
    xs = x_sample.reshape(Bs * Ts, D_MODEL)
    n_gates = 2 * M_HEADS

    tab_p = _rope_tables(jnp.arange(T, dtype=F32))
    tab_s = tuple(jnp.tile(t, (TOKEN_TILE // Ts, 1))
                  for t in _rope_tables(PAST_LEN + jnp.arange(Ts, dtype=F32)))

    pC, pn, pm, pk, pv, pconv = [], [], [], [], [], []
    sC, sn, sm, sk, sv, sconv = [], [], [], [], [], []
    for i in range(DEPTH):
        j = i // 2
        nw = norm_mix[i][None]
        if i % 2 == 0:
            w_in = jnp.pad(w_mlstm_in[j], ((0, 0), (0, LANES - n_gates))).astype(BF16)
            bg = jnp.pad(b_mlstm_gates[j], (0, LANES - n_gates))[None]
            hw = mlstm_head_norm[j][None]
            wo = w_mlstm_out[j].astype(BF16)

            proj, gates = _mlstm_inproj(xp, nw, w_in, bg, BF16)
            hs, C1, n1, m1 = _mlstm_prompt(proj.reshape(B, T, M_PROJ), gates.reshape(B, T, LANES), B, T)
            xp = _mlstm_out(hs.reshape(B * T, M_HV), proj, hw, wo, xp)
            pC.append(C1)
            pn.append(n1[:, :, 0])
            pm.append(m1[:, :, 0, 0])

            proj, gates = _mlstm_inproj(xs, nw, w_in, bg, F32)
            m0 = jnp.broadcast_to(state_mlstm_m[j][:, :, None, None], (Bs, M_HEADS, 1, LANES))
            hs, C2, n2, m2 = _mlstm_sample(proj, gates, state_mlstm_C[j], state_mlstm_n[j][:, :, None, :],
                                           m0, Bs, Ts)
            xs = _mlstm_out(hs, proj, hw, wo, xs)
            sC.append(C2)
            sn.append(n2[:, :, 0])
            sm.append(m2[:, :, 0, 0])
        else:
            w = w_swa_qkv[j].astype(BF16)
            qw = jnp.tile(swa_q_norm[j], LANES // S_HD)[None]
            kw = jnp.tile(swa_k_norm[j], LANES // S_HD)[None]
            wvt = w[:, S_HQ + S_HKV:].T
            sinks = swa_sinks[j]
            wo = w_swa_out[j].astype(BF16)

            q, k, v, vt = _swa_qkv(xp, nw, w, wvt, qw, kw, *tab_p)
            o = _swa_prompt(q.reshape(B, T, S_HQ), k.reshape(B, T, S_HKV), vt, sinks, B, T)
            xp = _proj_residual(o.reshape(B * T, S_HQ), wo, xp)
            pk.append(k.reshape(B, T, S_KV, S_HD)[:, T - WINDOW:])
            pv.append(v.reshape(B, T, S_KV, S_HD)[:, T - WINDOW:])

            q, k, v, _ = _swa_qkv(xs, nw, w, wvt, qw, kw, *tab_s)
            o, k2, v2 = _swa_sample(q, k, v, cache_swa_k[j].reshape(Bs, W, S_HKV),
                                    cache_swa_v[j].reshape(Bs, W, S_HKV), jnp.repeat(sinks, Ts)[None], Bs, Ts)
            xs = _proj_residual(o, wo, xs)
            sk.append(k2.reshape(Bs, W, S_KV, S_HD))
            sv.append(v2.reshape(Bs, W, S_KV, S_HD))

        nwf = norm_ffn[i][None]
        wg = w_ffn_up[i][:, :D_FF].astype(BF16)
        wu = w_ffn_up[i][:, D_FF:].astype(BF16)
        cw = ffn_conv_w[i]
        cb = ffn_conv_b[i][None]
        wd = w_ffn_down[i].astype(BF16)

        xp, gtail = _convglu_prompt(xp, nwf, wg, wu, cw, cb, wd, B, T)
        pconv.append(gtail[:, SUBLANES - (CONV_W - 1):])

        st = state_ffn_conv[i]
        e = jnp.pad(st, ((0, 0), (0, Ts - (CONV_W - 1)), (0, 0))).reshape(Bs * Ts, D_FF)
        xs, g = _convglu_sample(xs, nwf, wg, wu, cw, cb, wd, e, Ts)
        sconv.append(g.reshape(Bs, Ts, D_FF)[:, Ts - (CONV_W - 1):])

    return (xp.reshape(B, T, D_MODEL), xs.reshape(Bs, Ts, D_MODEL),
            jnp.stack(pC), jnp.stack(pn), jnp.stack(pm), jnp.stack(pk), jnp.stack(pv), jnp.stack(pconv),
            jnp.stack(sC), jnp.stack(sn), jnp.stack(sm), jnp.stack(sk), jnp.stack(sv), jnp.stack(sconv))
```

```python
import functools
import math

import jax
import jax.numpy as jnp
from jax import lax
from jax.experimental import pallas as pl
from jax.experimental.pallas import tpu as pltpu

F32 = jnp.float32
BF16 = jnp.bfloat16

D_MODEL = 1024
DEPTH = 4
PAST_LEN = 8192
M_HEADS = 4
M_DV = D_MODEL // M_HEADS
M_DK = M_DV // 2
M_HK = M_HEADS * M_DK
M_HV = M_HEADS * M_DV
M_PROJ = 2 * M_HK + 2 * M_HV
S_HD = 64
S_HEADS = D_MODEL // S_HD
S_KV = 4
S_GROUP = S_HEADS // S_KV
S_HQ = S_HEADS * S_HD
S_HKV = S_KV * S_HD
WINDOW = 128
ROT_DIM = S_HD // 4
ROPE_THETA = 500000.0
D_FF = 2816
CONV_W = 3
EPS = 1e-6

LANES = 128
SUBLANES = 8
VMEM_LIMIT = 56 * 1024 * 1024
TOKEN_TILE = 512
FF_CHUNK = 256
PROJ_CHUNK = 512
MLSTM_CHUNK = 256
SAMPLE_BLOCK = 8
SAMPLE_UNROLL = 2
SCORE_LOOKAHEAD = 16
NEG_INF = float("-inf")


def _params(*sem):
    return pltpu.CompilerParams(dimension_semantics=sem, vmem_limit_bytes=VMEM_LIMIT)


def _resident(shape):
    zeros = (0,) * len(shape)
    return pl.BlockSpec(shape, lambda *_: zeros, pipeline_mode=pl.Buffered(1))


def _rmsnorm(x, w):
    ms = jnp.mean(x * x, axis=-1, keepdims=True)
    return x * lax.rsqrt(ms + EPS) * w


def _dot(a, b):
    return jnp.dot(a, b, preferred_element_type=F32)


def _dot_nt(a, b):
    return lax.dot_general(a, b, (((1,), (1,)), ((), ())), preferred_element_type=F32)


def _dot_tn(a, b):
    return lax.dot_general(a, b, (((0,), (0,)), ((), ())), preferred_element_type=F32)


def _mlstm_inproj_kernel(x_ref, nw_ref, w_ref, bg_ref, proj_ref, gate_ref):
    hb = _rmsnorm(x_ref[...], nw_ref[...]).astype(BF16)
    for j in range(M_PROJ // PROJ_CHUNK):
        cols = slice(j * PROJ_CHUNK, (j + 1) * PROJ_CHUNK)
        r = _dot(hb, w_ref[:, cols])
        if j * PROJ_CHUNK == M_HK:
            r = r * (M_DK ** -0.5)
        proj_ref[:, cols] = r.astype(proj_ref.dtype)
    g = _dot(hb, w_ref[:, M_PROJ:]) + bg_ref[...]
    logf = -(jnp.maximum(-g, 0.0) + jnp.log1p(jnp.exp(-jnp.abs(g))))
    lane = lax.broadcasted_iota(jnp.int32, g.shape, 1)
    gate_ref[...] = jnp.where(lane < M_HEADS, g, logf)


def _mlstm_inproj(x, nw, w, bg, proj_dtype):
    n = x.shape[0]
    tm = TOKEN_TILE
    assert n % tm == 0 and M_HK == PROJ_CHUNK
    return pl.pallas_call(
        _mlstm_inproj_kernel,
        grid=(n // tm,),
        in_specs=[
            pl.BlockSpec((tm, D_MODEL), lambda i: (i, 0)),
            _resident((1, D_MODEL)),
            _resident((D_MODEL, M_PROJ + LANES)),
            _resident((1, LANES)),
        ],
        out_specs=[
            pl.BlockSpec((tm, M_PROJ), lambda i: (i, 0)),
            pl.BlockSpec((tm, LANES), lambda i: (i, 0)),
        ],
        out_shape=[
            jax.ShapeDtypeStruct((n, M_PROJ), proj_dtype),
            jax.ShapeDtypeStruct((n, LANES), F32),
        ],
        compiler_params=_params("parallel"),
        name="mlstm_inproj",
    )(x, nw, w, bg)


def _mlstm_chunks(units):
    L = units[0][0].shape[0]
    t_idx = lax.broadcasted_iota(jnp.int32, (L, L), 0)
    s_idx = lax.broadcasted_iota(jnp.int32, (L, L), 1)
    causal = s_idx <= t_idx
    eye = s_idx == t_idx

    def to_row(col):
        return jnp.sum(jnp.where(eye, col, 0.0), axis=0, keepdims=True)

    first = [(_dot_nt(q, k), _dot(q, C.astype(BF16))) for q, k, v, _, _, C, _, _ in units]
    mid = []
    for (q, k, v, i_col, f_col, C, n, m), (qk, qC) in zip(units, first):
        f_row = to_row(f_col)
        i_row = to_row(i_col)
        b_col = jnp.sum(jnp.where(causal, f_row, 0.0), axis=1, keepdims=True)
        b_row = to_row(b_col)
        d = jnp.where(causal, b_col - b_row + i_row, NEG_INF)
        m_inter = b_col + m
        m_t = jnp.maximum(m_inter, jnp.max(d, axis=1, keepdims=True))
        w_intra = jnp.exp(d - m_t) * qk
        w_inter = jnp.exp(m_inter - m_t)
        qn = jnp.sum(q.astype(F32) * n, axis=1, keepdims=True)
        nq = w_inter * qn + jnp.sum(w_intra, axis=1, keepdims=True)
        inv_den = 1.0 / jnp.maximum(jnp.abs(nq), jnp.exp(-m_t))
        m_new = m_t[L - 1:L, :]
        b_last = b_col[L - 1:L, :]
        w_state = jnp.exp(b_last - b_col + i_col - m_new)
        decay = jnp.exp(b_last + m - m_new)
        kw = k.astype(F32) * w_state
        n_new = decay * n + jnp.sum(kw, axis=0, keepdims=True)
        mid.append((w_intra.astype(BF16), kw.astype(BF16), w_inter, inv_den, decay, n_new, m_new))
    second = [(_dot(w_intra, u[2]), _dot_tn(kw, u[2])) for u, (w_intra, kw, *_) in zip(units, mid)]
    out = []
    for u, (_, qC), (_, _, w_inter, inv_den, decay, n_new, m_new), (wv, kv) in zip(units, first, mid, second):
        h = (w_inter * qC + wv) * inv_den
        out.append((h, decay * u[5] + kv, n_new, m_new))
    return out


def _mlstm_prompt_kernel(q_ref, k_ref, v_ref, g_ref, hs_ref, C_ref, n_ref, m_ref, *, chunk):
    T = q_ref.shape[1]
    C_ref[...] = jnp.zeros_like(C_ref)
    n_ref[...] = jnp.zeros_like(n_ref)
    m_ref[...] = jnp.zeros_like(m_ref)

    def body(c, carry):
        rows = pl.ds(pl.multiple_of(c * chunk, chunk), chunk)
        g = g_ref[0, rows, :]
        units = []
        for hd in range(M_HEADS):
            units.append((q_ref[0, rows, hd * M_DK:(hd + 1) * M_DK],
                          k_ref[0, rows, hd * M_DK:(hd + 1) * M_DK],
                          v_ref[0, rows, hd * M_DV:(hd + 1) * M_DV],
                          g[:, hd:hd + 1], g[:, M_HEADS + hd:M_HEADS + hd + 1],
                          C_ref[0, hd], n_ref[0, hd], m_ref[0, hd, :, 0:1]))
        for hd, (h, C_new, n_new, m_new) in enumerate(_mlstm_chunks(units)):
            hs_ref[0, rows, hd * M_DV:(hd + 1) * M_DV] = h.astype(hs_ref.dtype)
            C_ref[0, hd] = C_new
            n_ref[0, hd] = n_new
            m_ref[0, hd] = jnp.broadcast_to(m_new, (1, LANES))
        return carry

    lax.fori_loop(0, T // chunk, body, 0)


def _mlstm_prompt(proj, gates, B, T):
    chunk = MLSTM_CHUNK
    assert T % chunk == 0
    return pl.pallas_call(
        functools.partial(_mlstm_prompt_kernel, chunk=chunk),
        grid=(B,),
        in_specs=[
            pl.BlockSpec((1, T, M_HK), lambda b: (b, 0, 0)),
            pl.BlockSpec((1, T, M_HK), lambda b: (b, 0, 1)),
            pl.BlockSpec((1, T, M_HV), lambda b: (b, 0, 2 * M_HK // M_HV)),
            pl.BlockSpec((1, T, LANES), lambda b: (b, 0, 0)),
        ],
        out_specs=[
            pl.BlockSpec((1, T, M_HV), lambda b: (b, 0, 0)),
            pl.BlockSpec((1, M_HEADS, M_DK, M_DV), lambda b: (b, 0, 0, 0)),
            pl.BlockSpec((1, M_HEADS, 1, M_DK), lambda b: (b, 0, 0, 0)),
            pl.BlockSpec((1, M_HEADS, 1, LANES), lambda b: (b, 0, 0, 0)),
        ],
        out_shape=[
            jax.ShapeDtypeStruct((B, T, M_HV), BF16),
            jax.ShapeDtypeStruct((B, M_HEADS, M_DK, M_DV), F32),
            jax.ShapeDtypeStruct((B, M_HEADS, 1, M_DK), F32),
            jax.ShapeDtypeStruct((B, M_HEADS, 1, LANES), F32),
        ],
        compiler_params=_params("parallel"),
        name="mlstm_prompt",
    )(proj, proj, proj, gates)


def _mlstm_sample_kernel(q_ref, k_ref, v_ref, g_ref, C0_ref, n0_ref, m0_ref,
                         hs_ref, C_ref, n_ref, m_ref, *, T):
    nb = C0_ref.shape[0]

    def body(i, carry):
        units, where = [], []
        for s in range(SAMPLE_UNROLL):
            b = i * SAMPLE_UNROLL + s
            rows = pl.ds(pl.multiple_of(b * T, T), T)
            g = g_ref[rows, :]
            for hd in range(M_HEADS):
                units.append((q_ref[rows, hd * M_DK:(hd + 1) * M_DK].astype(BF16),
                              k_ref[rows, hd * M_DK:(hd + 1) * M_DK].astype(BF16),
                              v_ref[rows, hd * M_DV:(hd + 1) * M_DV].astype(BF16),
                              g[:, hd:hd + 1], g[:, M_HEADS + hd:M_HEADS + hd + 1],
                              C0_ref[b, hd], n0_ref[b, hd], m0_ref[b, hd, :, 0:1]))
                where.append((b, rows, hd))
        for (b, rows, hd), (h, C_new, n_new, m_new) in zip(where, _mlstm_chunks(units)):
            hs_ref[rows, hd * M_DV:(hd + 1) * M_DV] = h.astype(hs_ref.dtype)
            C_ref[b, hd] = C_new
            n_ref[b, hd] = n_new
            m_ref[b, hd] = jnp.broadcast_to(m_new, (1, LANES))
        return carry

    lax.fori_loop(0, nb // SAMPLE_UNROLL, body, 0)


def _mlstm_sample(proj, gates, C0, layer, n0, m0, B, T):
    nb = SAMPLE_BLOCK
    assert B % nb == 0 and T % SUBLANES == 0
    rows = nb * T
    return pl.pallas_call(
        functools.partial(_mlstm_sample_kernel, T=T),
        grid=(B // nb,),
        in_specs=[
            pl.BlockSpec((rows, M_HK), lambda i: (i, 0)),
            pl.BlockSpec((rows, M_HK), lambda i: (i, 1)),
            pl.BlockSpec((rows, M_HV), lambda i: (i, 2 * M_HK // M_HV)),
            pl.BlockSpec((rows, LANES), lambda i: (i, 0)),
            pl.BlockSpec((None, nb, M_HEADS, M_DK, M_DV), lambda i: (layer, i, 0, 0, 0)),
            pl.BlockSpec((nb, M_HEADS, 1, M_DK), lambda i: (i, 0, 0, 0)),
            pl.BlockSpec((nb, M_HEADS, 1, LANES), lambda i: (i, 0, 0, 0)),
        ],
        out_specs=[
            pl.BlockSpec((rows, M_HV), lambda i: (i, 0)),
            pl.BlockSpec((nb, M_HEADS, M_DK, M_DV), lambda i: (i, 0, 0, 0)),
            pl.BlockSpec((nb, M_HEADS, 1, M_DK), lambda i: (i, 0, 0, 0)),
            pl.BlockSpec((nb, M_HEADS, 1, LANES), lambda i: (i, 0, 0, 0)),
        ],
        out_shape=[
            jax.ShapeDtypeStruct((B * T, M_HV), BF16),
            jax.ShapeDtypeStruct((B, M_HEADS, M_DK, M_DV), F32),
            jax.ShapeDtypeStruct((B, M_HEADS, 1, M_DK), F32),
            jax.ShapeDtypeStruct((B, M_HEADS, 1, LANES), F32),
        ],
        compiler_params=_params("parallel"),
        name="mlstm_sample",
    )(proj, proj, proj, gates, C0, n0, m0)


def _mlstm_out_kernel(hs_ref, o_ref, hw_ref, w_ref, x_ref, y_ref):
    hs = hs_ref[...].astype(F32)
    parts = []
    for hd in range(M_HEADS):
        seg = hs[:, hd * M_DV:(hd + 1) * M_DV]
        ms = jnp.mean(seg * seg, axis=-1, keepdims=True)
        parts.append(seg * lax.rsqrt(ms + EPS))
    hn = jnp.concatenate(parts, axis=1) * hw_ref[...]
    a = hn * jax.nn.sigmoid(o_ref[...].astype(F32))
    y_ref[...] = x_ref[...] + _dot(a.astype(BF16), w_ref[...])


def _mlstm_out(hs, proj, hw, w, x):
    n = x.shape[0]
    tm = TOKEN_TILE
    o_block = (2 * M_HK + M_HV) // M_HV
    return pl.pallas_call(
        _mlstm_out_kernel,
        grid=(n // tm,),
        in_specs=[
            pl.BlockSpec((tm, M_HV), lambda i: (i, 0)),
            pl.BlockSpec((tm, M_HV), lambda i: (i, o_block)),
            _resident((1, M_HV)),
            _resident((M_HV, D_MODEL)),
            pl.BlockSpec((tm, D_MODEL), lambda i: (i, 0)),
        ],
        out_specs=pl.BlockSpec((tm, D_MODEL), lambda i: (i, 0)),
        out_shape=jax.ShapeDtypeStruct((n, D_MODEL), F32),
        compiler_params=_params("parallel"),
        name="mlstm_out",
    )(hs, proj, hw, w, x)


def _head_norm_rope(y, w, cos, s_lo, s_hi):
    lane = lax.broadcasted_iota(jnp.int32, y.shape, 1)
    first = lane < S_HD
    sq = y * y
    lo = jnp.sum(jnp.where(first, sq, 0.0), axis=-1, keepdims=True)
    hi = jnp.sum(jnp.where(first, 0.0, sq), axis=-1, keepdims=True)
    ms = jnp.where(first, lo, hi) * (1.0 / S_HD)
    z = y * lax.rsqrt(ms + EPS) * w
    half = ROT_DIM // 2
    return z * cos + pltpu.roll(z, LANES - half, 1) * s_lo + pltpu.roll(z, half, 1) * s_hi


def _swa_qkv_kernel(x_ref, nw_ref, w_ref, wvt_ref, qw_ref, kw_ref, cos_ref, slo_ref, shi_ref,
                    q_ref, k_ref, v_ref, vt_ref):
    hb = _rmsnorm(x_ref[...], nw_ref[...]).astype(BF16)
    cos, s_lo, s_hi = cos_ref[...], slo_ref[...], shi_ref[...]
    for j in range(S_HQ // LANES):
        cols = slice(j * LANES, (j + 1) * LANES)
        y = _head_norm_rope(_dot(hb, w_ref[:, cols]), qw_ref[...], cos, s_lo, s_hi)
        q_ref[:, cols] = (y * (S_HD ** -0.5)).astype(q_ref.dtype)
    for j in range(S_HKV // LANES):
        cols = slice(j * LANES, (j + 1) * LANES)
        wcols = slice(S_HQ + j * LANES, S_HQ + (j + 1) * LANES)
        k_ref[:, cols] = _head_norm_rope(_dot(hb, w_ref[:, wcols]), kw_ref[...], cos, s_lo, s_hi)
    v_ref[...] = _dot(hb, w_ref[:, S_HQ + S_HKV:])
    vt_ref[...] = _dot_nt(wvt_ref[...], hb).astype(vt_ref.dtype)


def _swa_qkv(x, nw, w, wvt, qw, kw, cos, s_lo, s_hi):
    n = x.shape[0]
    tm = TOKEN_TILE
    tab_blocks = cos.shape[0] // tm
    tab = pl.BlockSpec((tm, LANES), lambda i: (i % tab_blocks, 0))
    return pl.pallas_call(
        _swa_qkv_kernel,
        grid=(n // tm,),
        in_specs=[
            pl.BlockSpec((tm, D_MODEL), lambda i: (i, 0)),
            _resident((1, D_MODEL)),
            _resident((D_MODEL, S_HQ + 2 * S_HKV)),
            _resident((S_HKV, D_MODEL)),
            _resident((1, LANES)),
            _resident((1, LANES)),
            tab, tab, tab,
        ],
        out_specs=[
            pl.BlockSpec((tm, S_HQ), lambda i: (i, 0)),
            pl.BlockSpec((tm, S_HKV), lambda i: (i, 0)),
            pl.BlockSpec((tm, S_HKV), lambda i: (i, 0)),
            pl.BlockSpec((S_HKV, tm), lambda i: (0, i)),
        ],
        out_shape=[
            jax.ShapeDtypeStruct((n, S_HQ), BF16),
            jax.ShapeDtypeStruct((n, S_HKV), F32),
            jax.ShapeDtypeStruct((n, S_HKV), F32),
            jax.ShapeDtypeStruct((S_HKV, n), BF16),
        ],
        compiler_params=_params("parallel"),
        name="swa_qkv",
    )(x, nw, w, wvt, qw, kw, cos, s_lo, s_hi)


def _rope_tables(pos):
    half = ROT_DIM // 2
    inv = ROPE_THETA ** (-jnp.arange(half, dtype=F32) * 2.0 / ROT_DIM)
    ang = pos[:, None] * inv[None, :]
    cos, sin = jnp.cos(ang), jnp.sin(ang)
    n = pos.shape[0]
    pad = jnp.zeros((n, S_HD - ROT_DIM), F32)
    zero = jnp.zeros((n, half), F32)
    cos_h = jnp.concatenate([cos, cos, pad + 1.0], axis=1)
    lo_h = jnp.concatenate([-sin, zero, pad], axis=1)
    hi_h = jnp.concatenate([zero, sin, pad], axis=1)
    two = lambda t: jnp.concatenate([t, t], axis=1)
    return two(cos_h), two(lo_h), two(hi_h)


def _softmax_over_keys(s, mask, sink):
    s = jnp.where(mask, s, NEG_INF)
    m = jnp.maximum(jnp.max(s, axis=0, keepdims=True), sink)
    p = jnp.exp(s - m)
    denom = jnp.sum(p, axis=0, keepdims=True) + jnp.exp(sink - m)
    return p * (1.0 / denom)


def _swa_prompt_kernel(sink_ref, q_ref, kp_ref, ko_ref, vtp_ref, vto_ref, o_ref):
    blk = pl.program_id(1)
    W = WINDOW
    kf = jnp.concatenate([kp_ref[0], ko_ref[0]], axis=0)
    vt = jnp.concatenate([vtp_ref[...], vto_ref[...]], axis=1)
    j = lax.broadcasted_iota(jnp.int32, (2 * W, W), 0)
    rel = j - lax.broadcasted_iota(jnp.int32, (2 * W, W), 1)
    mask = (rel >= 0) & (rel <= W) & ((j >= W) | (blk > 0))
    low = lax.broadcasted_iota(jnp.int32, (2 * W, LANES), 1) < S_HD
    q = q_ref[0]
    halves = LANES // S_HD
    k_padded = {}
    for c in range(S_HKV // LANES):
        kc = kf[:, c * LANES:(c + 1) * LANES]
        kr = pltpu.roll(kc, S_HD, 1)
        for half in range(halves):
            k_lo = jnp.where(low, kc if half == 0 else kr, 0.0).astype(BF16)
            k_hi = jnp.where(low, 0.0, kr if half == 0 else kc).astype(BF16)
            k_padded[c * halves + half] = (k_lo, k_hi)

    def scores(h):
        pair = h // halves
        return _dot_nt(k_padded[h // S_GROUP][h % halves], q[:, pair * LANES:(pair + 1) * LANES])

    pending = [scores(h) for h in range(SCORE_LOOKAHEAD)]
    outs = []
    for h in range(S_HEADS):
        g = h // S_GROUP
        sink = sink_ref[h]
        s = jnp.where(mask, pending.pop(0), NEG_INF)
        m = jnp.maximum(jnp.max(s, axis=0, keepdims=True), sink)
        p = jnp.exp(s - m)
        denom = jnp.sum(p, axis=0, keepdims=True) + jnp.exp(sink - m)
        if h + SCORE_LOOKAHEAD < S_HEADS:
            pending.append(scores(h + SCORE_LOOKAHEAD))
        outs.append(_dot(vt[g * S_HD:(g + 1) * S_HD, :], p.astype(BF16)) * (1.0 / denom))
        if len(outs) == halves:
            pair = h // halves
            o_pair = jnp.concatenate(outs, axis=0).T
            o_ref[0, :, pair * LANES:(pair + 1) * LANES] = o_pair.astype(o_ref.dtype)
            outs = []


def _swa_prompt(q, k, vt, sinks, B, T):
    W = WINDOW
    nb = T // W
    prev = lambda b, i: (b, jnp.maximum(i - 1, 0), 0)
    own = lambda b, i: (b, i, 0)
    return pl.pallas_call(
        _swa_prompt_kernel,
        grid=(B, nb),
        in_specs=[
            pl.BlockSpec(memory_space=pltpu.SMEM),
            pl.BlockSpec((1, W, S_HQ), own),
            pl.BlockSpec((1, W, S_HKV), prev),
            pl.BlockSpec((1, W, S_HKV), own),
            pl.BlockSpec((S_HKV, W), lambda b, i: (0, b * nb + jnp.maximum(i - 1, 0))),
            pl.BlockSpec((S_HKV, W), lambda b, i: (0, b * nb + i)),
        ],
        out_specs=pl.BlockSpec((1, W, S_HQ), own),
        out_shape=jax.ShapeDtypeStruct((B, T, S_HQ), BF16),
        compiler_params=_params("parallel", "parallel"),
        name="swa_prompt",
    )(sinks, q, k, k, vt, vt)


SAMPLE_KEY_PAD = 16


def _swa_sample_kernel(q_ref, kn_ref, vn_ref, kc_ref, vc_ref, sink_ref, o_ref, ko_ref, vo_ref, *, T):
    nb, W = kc_ref.shape[0], kc_ref.shape[1]
    J = W + SAMPLE_KEY_PAD
    j = lax.broadcasted_iota(jnp.int32, (J, LANES), 0)
    t = lax.rem(lax.broadcasted_iota(jnp.int32, (J, LANES), 1), T)
    rel = t + W - j
    mask = (rel >= 0) & (rel <= WINDOW)
    low = lax.broadcasted_iota(jnp.int32, (T, LANES), 1) < S_HD
    zero = jnp.zeros((T, LANES), F32)
    kpad = jnp.zeros((SAMPLE_KEY_PAD - T, S_HKV), F32)
    sink = sink_ref[...]
    halves = LANES // S_HD
    for b in range(nb):
        rows = slice(b * T, (b + 1) * T)
        kn, vn, kc, vc = kn_ref[rows, :], vn_ref[rows, :], kc_ref[b], vc_ref[b]
        ko_ref[b, :W - T] = kc[T:]
        ko_ref[b, W - T:] = kn
        vo_ref[b, :W - T] = vc[T:]
        vo_ref[b, W - T:] = vn
        kall = jnp.concatenate([kc, kn, kpad], axis=0).astype(BF16)
        vall = jnp.concatenate([vc, vn, kpad], axis=0).astype(BF16)
        qf = q_ref[rows, :].astype(F32)
        blocks = []
        for h in range(S_HEADS):
            g = h // S_GROUP
            src = qf[:, (h // halves) * LANES:(h // halves + 1) * LANES]
            if h % halves != g % halves:
                src = pltpu.roll(src, S_HD, 1)
            src = jnp.where(low if g % halves == 0 else ~low, src, 0.0)
            blocks.append(jnp.concatenate(
                [src if c == g // halves else zero for c in range(S_HKV // LANES)], axis=1))
        qblk = jnp.concatenate(blocks, axis=0).astype(BF16)
        pn = _softmax_over_keys(_dot_nt(kall, qblk), mask, sink)
        o_full = _dot_tn(pn.astype(BF16), vall)
        for pair in range(S_HEADS // halves):
            pieces = []
            for hh in range(halves):
                h = pair * halves + hh
                g = h // S_GROUP
                piece = o_full[h * T:(h + 1) * T, (g // halves) * LANES:(g // halves + 1) * LANES]
                if g % halves != hh:
                    piece = pltpu.roll(piece, S_HD, 1)
                pieces.append(piece)
            o_ref[rows, pair * LANES:(pair + 1) * LANES] = jnp.where(low, pieces[0], pieces[1]).astype(o_ref.dtype)


def _swa_sample(q, kn, vn, kc, vc, layer, sinks, B, T):
    nb = SAMPLE_BLOCK
    W = kc.shape[2]
    rows = nb * T
    assert S_HEADS * T == LANES and LANES // S_HD == 2 and T <= SAMPLE_KEY_PAD
    return pl.pallas_call(
        functools.partial(_swa_sample_kernel, T=T),
        grid=(B // nb,),
        in_specs=[
            pl.BlockSpec((rows, S_HQ), lambda i: (i, 0)),
            pl.BlockSpec((rows, S_HKV), lambda i: (i, 0)),
            pl.BlockSpec((rows, S_HKV), lambda i: (i, 0)),
            pl.BlockSpec((None, nb, W, S_HKV), lambda i: (layer, i, 0, 0)),
            pl.BlockSpec((None, nb, W, S_HKV), lambda i: (layer, i, 0, 0)),
            _resident((1, LANES)),
        ],
        out_specs=[
            pl.BlockSpec((rows, S_HQ), lambda i: (i, 0)),
            pl.BlockSpec((nb, W, S_HKV), lambda i: (i, 0, 0)),
            pl.BlockSpec((nb, W, S_HKV), lambda i: (i, 0, 0)),
        ],
        out_shape=[
            jax.ShapeDtypeStruct((B * T, S_HQ), BF16),
            jax.ShapeDtypeStruct((B, W, S_HKV), F32),
            jax.ShapeDtypeStruct((B, W, S_HKV), F32),
        ],
        compiler_params=_params("parallel"),
        name="swa_sample",
    )(q, kn, vn, kc, vc, sinks)


def _proj_residual_kernel(a_ref, w_ref, x_ref, y_ref):
    y_ref[...] = x_ref[...] + _dot(a_ref[...], w_ref[...])


def _proj_residual(a, w, x):
    n = x.shape[0]
    tm = TOKEN_TILE
    return pl.pallas_call(
        _proj_residual_kernel,
        grid=(n // tm,),
        in_specs=[
            pl.BlockSpec((tm, a.shape[1]), lambda i: (i, 0)),
            _resident(w.shape),
            pl.BlockSpec((tm, D_MODEL), lambda i: (i, 0)),
        ],
        out_specs=pl.BlockSpec((tm, D_MODEL), lambda i: (i, 0)),
        out_shape=jax.ShapeDtypeStruct((n, D_MODEL), F32),
        compiler_params=_params("parallel"),
        name="proj_residual",
    )(a, w, x)


def _gelu(x):
    return 0.5 * x * (1.0 + lax.erf(x * (2.0 ** -0.5)))


def _convglu_body(x_ref, nw_ref, wg_ref, wu_ref, cw_ref, cb_ref, wd_ref, y_ref, act_ref, prev_rows):
    hb = _rmsnorm(x_ref[...], nw_ref[...]).astype(BF16)
    for f in range(D_FF // FF_CHUNK):
        cols = slice(f * FF_CHUNK, (f + 1) * FF_CHUNK)
        g = _dot(hb, wg_ref[:, cols])
        u = _dot(hb, wu_ref[:, cols])
        g1, g2 = prev_rows(cols, g)
        gc = cb_ref[:, cols] + cw_ref[0:1, cols] * g2 + cw_ref[1:2, cols] * g1 + cw_ref[2:3, cols] * g
        act_ref[:, cols] = (_gelu(gc) * u).astype(BF16)
    y_ref[...] = x_ref[...] + _dot(act_ref[...], wd_ref[...])


def _convglu_prompt_kernel(x_ref, nw_ref, wg_ref, wu_ref, cw_ref, cb_ref, wd_ref,
                           y_ref, g_ref, act_ref, tail_ref, *, tiles_per_seq):
    tm = x_ref.shape[0]
    row = lax.broadcasted_iota(jnp.int32, (tm, FF_CHUNK), 0)

    @pl.when(pl.program_id(0) % tiles_per_seq == 0)
    def _():
        tail_ref[...] = jnp.zeros_like(tail_ref)

    def prev_rows(cols, g):
        tail = tail_ref[:, cols]
        last, last2 = tail[SUBLANES - 1:SUBLANES, :], tail[SUBLANES - 2:SUBLANES - 1, :]
        g1 = jnp.where(row < 1, last, pltpu.roll(g, 1, 0))
        g2 = jnp.where(row < 2, jnp.where(row < 1, last2, last), pltpu.roll(g, 2, 0))
        tail_ref[:, cols] = g[tm - SUBLANES:, :]
        return g1, g2

    _convglu_body(x_ref, nw_ref, wg_ref, wu_ref, cw_ref, cb_ref, wd_ref, y_ref, act_ref, prev_rows)
    g_ref[0] = tail_ref[...]


def _convglu_sample_kernel(x_ref, nw_ref, wg_ref, wu_ref, cw_ref, cb_ref, wd_ref, e_ref,
                           y_ref, g_ref, act_ref, *, seq_len):
    tm = x_ref.shape[0]
    tpos = lax.broadcasted_iota(jnp.int32, (tm, FF_CHUNK), 0) % seq_len

    def prev_rows(cols, g):
        g_ref[:, cols] = g
        e = e_ref[:, cols]
        g1 = jnp.where(tpos < 1, pltpu.roll(e, tm - 1, 0), pltpu.roll(g, 1, 0))
        g2 = jnp.where(tpos < 2, e, pltpu.roll(g, 2, 0))
        return g1, g2

    _convglu_body(x_ref, nw_ref, wg_ref, wu_ref, cw_ref, cb_ref, wd_ref, y_ref, act_ref, prev_rows)


def _convglu_prompt(x, nw, wg, wu, cw, cb, wd, B, T):
    n = x.shape[0]
    tm = TOKEN_TILE
    assert T % tm == 0
    tiles_per_seq = T // tm

    return pl.pallas_call(
        functools.partial(_convglu_prompt_kernel, tiles_per_seq=tiles_per_seq),
        grid=(n // tm,),
        in_specs=[
            pl.BlockSpec((tm, D_MODEL), lambda i: (i, 0)),
            _resident((1, D_MODEL)),
            _resident((D_MODEL, D_FF)),
            _resident((D_MODEL, D_FF)),
            _resident((CONV_W, D_FF)),
            _resident((1, D_FF)),
            _resident((D_FF, D_MODEL)),
        ],
        out_specs=[
            pl.BlockSpec((tm, D_MODEL), lambda i: (i, 0)),
            pl.BlockSpec((1, SUBLANES, D_FF), lambda i: (i // tiles_per_seq, 0, 0)),
        ],
        out_shape=[
            jax.ShapeDtypeStruct((n, D_MODEL), F32),
            jax.ShapeDtypeStruct((B, SUBLANES, D_FF), F32),
        ],
        scratch_shapes=[pltpu.VMEM((tm, D_FF), BF16), pltpu.VMEM((SUBLANES, D_FF), F32)],
        compiler_params=_params("arbitrary"),
        name="convglu_prompt",
    )(x, nw, wg, wu, cw, cb, wd)


def _convglu_sample(x, nw, wg, wu, cw, cb, wd, e, T):
    n = x.shape[0]
    tm = TOKEN_TILE
    assert tm % T == 0 and T >= CONV_W - 1 and CONV_W == 3

    return pl.pallas_call(
        functools.partial(_convglu_sample_kernel, seq_len=T),
        grid=(n // tm,),
        in_specs=[
            pl.BlockSpec((tm, D_MODEL), lambda i: (i, 0)),
            _resident((1, D_MODEL)),
            _resident((D_MODEL, D_FF)),
            _resident((D_MODEL, D_FF)),
            _resident((CONV_W, D_FF)),
            _resident((1, D_FF)),
            _resident((D_FF, D_MODEL)),
            pl.BlockSpec((tm, D_FF), lambda i: (i, 0)),
        ],
        out_specs=[
            pl.BlockSpec((tm, D_MODEL), lambda i: (i, 0)),
            pl.BlockSpec((tm, D_FF), lambda i: (i, 0)),
        ],
        out_shape=[
            jax.ShapeDtypeStruct((n, D_MODEL), F32),
            jax.ShapeDtypeStruct((n, D_FF), F32),
        ],
        scratch_shapes=[pltpu.VMEM((tm, D_FF), BF16)],
        compiler_params=_params("parallel"),
        name="convglu_sample",
    )(x, nw, wg, wu, cw, cb, wd, e)


def kernel(x_prompt, x_sample, state_mlstm_C, state_mlstm_n, state_mlstm_m, cache_swa_k, cache_swa_v,
           state_ffn_conv, norm_mix, norm_ffn, w_mlstm_in, b_mlstm_gates, mlstm_head_norm, w_mlstm_out,
           w_swa_qkv, swa_q_norm, swa_k_norm, swa_sinks, w_swa_out, w_ffn_up, ffn_conv_w, ffn_conv_b,
           w_ffn_down):
    B, T, _ = x_prompt.shape
    Bs, Ts, _ = x_sample.shape
    W = cache_swa_k.shape[2]
    xp = x_prompt.reshape(B * T, D_MODEL)
    xs = x_sample.reshape(Bs * Ts, D_MODEL)
    n_gates = 2 * M_HEADS
    cache_k = cache_swa_k.reshape(cache_swa_k.shape[0], Bs, W, S_HKV)
    cache_v = cache_swa_v.reshape(cache_swa_v.shape[0], Bs, W, S_HKV)

    tab_p = _rope_tables(jnp.arange(T, dtype=F32))
    tab_s = tuple(jnp.tile(t, (TOKEN_TILE // Ts, 1))
                  for t in _rope_tables(PAST_LEN + jnp.arange(Ts, dtype=F32)))

    pC, pn, pm, pk, pv, pconv = [], [], [], [], [], []
    sC, sn, sm, sk, sv, sconv = [], [], [], [], [], []
    for i in range(DEPTH):
        j = i // 2
        nw = norm_mix[i][None]
        if i % 2 == 0:
            w_in = jnp.pad(w_mlstm_in[j], ((0, 0), (0, LANES - n_gates))).astype(BF16)
            bg = jnp.pad(b_mlstm_gates[j], (0, LANES - n_gates))[None]
            hw = mlstm_head_norm[j][None]
            wo = w_mlstm_out[j].astype(BF16)

            proj, gates = _mlstm_inproj(xp, nw, w_in, bg, BF16)
            hs, C1, n1, m1 = _mlstm_prompt(proj.reshape(B, T, M_PROJ), gates.reshape(B, T, LANES), B, T)
            xp = _mlstm_out(hs.reshape(B * T, M_HV), proj, hw, wo, xp)
            pC.append(C1)
            pn.append(n1[:, :, 0])
            pm.append(m1[:, :, 0, 0])

            proj, gates = _mlstm_inproj(xs, nw, w_in, bg, F32)
            m0 = jnp.broadcast_to(state_mlstm_m[j][:, :, None, None], (Bs, M_HEADS, 1, LANES))
            hs, C2, n2, m2 = _mlstm_sample(proj, gates, state_mlstm_C, j, state_mlstm_n[j][:, :, None, :],
                                           m0, Bs, Ts)
            xs = _mlstm_out(hs, proj, hw, wo, xs)
            sC.append(C2)
            sn.append(n2[:, :, 0])
            sm.append(m2[:, :, 0, 0])
        else:
            w = w_swa_qkv[j].astype(BF16)
            qw = jnp.tile(swa_q_norm[j], LANES // S_HD)[None]
            kw = jnp.tile(swa_k_norm[j], LANES // S_HD)[None]
            wvt = w[:, S_HQ + S_HKV:].T
            sinks = swa_sinks[j]
            wo = w_swa_out[j].astype(BF16)

            q, k, v, vt = _swa_qkv(xp, nw, w, wvt, qw, kw, *tab_p)
            o = _swa_prompt(q.reshape(B, T, S_HQ), k.reshape(B, T, S_HKV), vt, sinks, B, T)
            xp = _proj_residual(o.reshape(B * T, S_HQ), wo, xp)
            pk.append(k.reshape(B, T, S_KV, S_HD)[:, T - WINDOW:])
            pv.append(v.reshape(B, T, S_KV, S_HD)[:, T - WINDOW:])

            q, k, v, _ = _swa_qkv(xs, nw, w, wvt, qw, kw, *tab_s)
            o, k2, v2 = _swa_sample(q, k, v, cache_k, cache_v, j, jnp.repeat(sinks, Ts)[None], Bs, Ts)
            xs = _proj_residual(o, wo, xs)
            sk.append(k2.reshape(Bs, W, S_KV, S_HD))
            sv.append(v2.reshape(Bs, W, S_KV, S_HD))

        nwf = norm_ffn[i][None]
        wg = w_ffn_up[i][:, :D_FF].astype(BF16)
        wu = w_ffn_up[i][:, D_FF:].astype(BF16)
        cw = ffn_conv_w[i]
        cb = ffn_conv_b[i][None]
        wd = w_ffn_down[i].astype(BF16)

        xp, gtail = _convglu_prompt(xp, nwf, wg, wu, cw, cb, wd, B, T)
        pconv.append(gtail[:, SUBLANES - (CONV_W - 1):])

        st = state_ffn_conv[i]
        e = jnp.pad(st, ((0, 0), (0, Ts - (CONV_W - 1)), (0, 0))).reshape(Bs * Ts, D_FF)
        xs, g = _convglu_sample(xs, nwf, wg, wu, cw, cb, wd, e, Ts)
        sconv.append(g.reshape(Bs, Ts, D_FF)[:, Ts - (CONV_W - 1):])

    return (xp.reshape(B, T, D_MODEL), xs.reshape(Bs, Ts, D_MODEL),
            jnp.stack(pC), jnp.stack(pn), jnp.stack(pm), jnp.stack(pk), jnp.stack(pv), jnp.stack(pconv),
            jnp.stack(sC), jnp.stack(sn), jnp.stack(sm), jnp.stack(sk), jnp.stack(sv), jnp.stack(sconv))
```

```python
import functools
import math

import jax
import jax.numpy as jnp
from jax import lax
from jax.experimental import pallas as pl
from jax.experimental.pallas import tpu as pltpu

F32 = jnp.float32
BF16 = jnp.bfloat16

D_MODEL = 1024
DEPTH = 4
PAST_LEN = 8192
M_HEADS = 4
M_DV = D_MODEL // M_HEADS
M_DK = M_DV // 2
M_HK = M_HEADS * M_DK
M_HV = M_HEADS * M_DV
M_PROJ = 2 * M_HK + 2 * M_HV
S_HD = 64
S_HEADS = D_MODEL // S_HD
S_KV = 4
S_GROUP = S_HEADS // S_KV
S_HQ = S_HEADS * S_HD
S_HKV = S_KV * S_HD
WINDOW = 128
ROT_DIM = S_HD // 4
ROPE_THETA = 500000.0
D_FF = 2816
CONV_W = 3
EPS = 1e-6

LANES = 128
SUBLANES = 8
VMEM_LIMIT = 56 * 1024 * 1024
TOKEN_TILE = 512
FF_CHUNK = 256
PROJ_CHUNK = 512
MLSTM_CHUNK = 256
SAMPLE_BLOCK = 8
SAMPLE_UNROLL = 2
SCORE_LOOKAHEAD = 16
NEG_INF = float("-inf")


def _params(*sem):
    return pltpu.CompilerParams(dimension_semantics=sem, vmem_limit_bytes=VMEM_LIMIT)


def _resident(shape):
    zeros = (0,) * len(shape)
    return pl.BlockSpec(shape, lambda *_: zeros, pipeline_mode=pl.Buffered(1))


def _rmsnorm(x, w):
    ms = jnp.mean(x * x, axis=-1, keepdims=True)
    return x * lax.rsqrt(ms + EPS) * w


def _dot(a, b):
    return jnp.dot(a, b, preferred_element_type=F32)


def _dot_nt(a, b):
    return lax.dot_general(a, b, (((1,), (1,)), ((), ())), preferred_element_type=F32)


def _dot_tn(a, b):
    return lax.dot_general(a, b, (((0,), (0,)), ((), ())), preferred_element_type=F32)


def _mlstm_inproj_kernel(x_ref, nw_ref, w_ref, bg_ref, proj_ref, gate_ref):
    hb = _rmsnorm(x_ref[...], nw_ref[...]).astype(BF16)
    for j in range(M_PROJ // PROJ_CHUNK):
        cols = slice(j * PROJ_CHUNK, (j + 1) * PROJ_CHUNK)
        r = _dot(hb, w_ref[:, cols])
        if j * PROJ_CHUNK == M_HK:
            r = r * (M_DK ** -0.5)
        proj_ref[:, cols] = r.astype(proj_ref.dtype)
    g = _dot(hb, w_ref[:, M_PROJ:]) + bg_ref[...]
    logf = -(jnp.maximum(-g, 0.0) + jnp.log1p(jnp.exp(-jnp.abs(g))))
    lane = lax.broadcasted_iota(jnp.int32, g.shape, 1)
    gate_ref[...] = jnp.where(lane < M_HEADS, g, logf)


def _mlstm_inproj(x, nw, w, bg, proj_dtype):
    n = x.shape[0]
    tm = TOKEN_TILE
    assert n % tm == 0 and M_HK == PROJ_CHUNK
    return pl.pallas_call(
        _mlstm_inproj_kernel,
        grid=(n // tm,),
        in_specs=[
            pl.BlockSpec((tm, D_MODEL), lambda i: (i, 0)),
            _resident((1, D_MODEL)),
            _resident((D_MODEL, M_PROJ + LANES)),
            _resident((1, LANES)),
        ],
        out_specs=[
            pl.BlockSpec((tm, M_PROJ), lambda i: (i, 0)),
            pl.BlockSpec((tm, LANES), lambda i: (i, 0)),
        ],
        out_shape=[
            jax.ShapeDtypeStruct((n, M_PROJ), proj_dtype),
            jax.ShapeDtypeStruct((n, LANES), F32),
        ],
        compiler_params=_params("parallel"),
        name="mlstm_inproj",
    )(x, nw, w, bg)


def _mlstm_chunks(units):
    L = units[0][0].shape[0]
    t_idx = lax.broadcasted_iota(jnp.int32, (L, L), 0)
    s_idx = lax.broadcasted_iota(jnp.int32, (L, L), 1)
    causal = s_idx <= t_idx
    eye = s_idx == t_idx

    def to_row(col):
        return jnp.sum(jnp.where(eye, col, 0.0), axis=0, keepdims=True)

    first = [(_dot_nt(q, k), _dot(q, C.astype(BF16))) for q, k, v, _, _, C, _, _ in units]
    mid = []
    for (q, k, v, i_col, f_col, C, n, m), (qk, qC) in zip(units, first):
        f_row = to_row(f_col)
        i_row = to_row(i_col)
        b_col = jnp.sum(jnp.where(causal, f_row, 0.0), axis=1, keepdims=True)
        b_row = to_row(b_col)
        d = jnp.where(causal, b_col - b_row + i_row, NEG_INF)
        m_inter = b_col + m
        m_t = jnp.maximum(m_inter, jnp.max(d, axis=1, keepdims=True))
        w_intra = jnp.exp(d - m_t) * qk
        w_inter = jnp.exp(m_inter - m_t)
        qn = jnp.sum(q.astype(F32) * n, axis=1, keepdims=True)
        nq = w_inter * qn + jnp.sum(w_intra, axis=1, keepdims=True)
        inv_den = 1.0 / jnp.maximum(jnp.abs(nq), jnp.exp(-m_t))
        m_new = m_t[L - 1:L, :]
        b_last = b_col[L - 1:L, :]
        w_state = jnp.exp(b_last - b_col + i_col - m_new)
        decay = jnp.exp(b_last + m - m_new)
        kw = k.astype(F32) * w_state
        n_new = decay * n + jnp.sum(kw, axis=0, keepdims=True)
        mid.append((w_intra.astype(BF16), kw.astype(BF16), w_inter, inv_den, decay, n_new, m_new))
    second = [(_dot(w_intra, u[2]), _dot_tn(kw, u[2])) for u, (w_intra, kw, *_) in zip(units, mid)]
    out = []
    for u, (_, qC), (_, _, w_inter, inv_den, decay, n_new, m_new), (wv, kv) in zip(units, first, mid, second):
        h = (w_inter * qC + wv) * inv_den
        out.append((h, decay * u[5] + kv, n_new, m_new))
    return out


def _mlstm_prompt_kernel(q_ref, k_ref, v_ref, g_ref, hs_ref, C_ref, n_ref, m_ref, *, chunk):
    T = q_ref.shape[1]
    C_ref[...] = jnp.zeros_like(C_ref)
    n_ref[...] = jnp.zeros_like(n_ref)
    m_ref[...] = jnp.zeros_like(m_ref)

    def body(c, carry):
        rows = pl.ds(pl.multiple_of(c * chunk, chunk), chunk)
        g = g_ref[0, rows, :]
        units = []
        for hd in range(M_HEADS):
            units.append((q_ref[0, rows, hd * M_DK:(hd + 1) * M_DK],
                          k_ref[0, rows, hd * M_DK:(hd + 1) * M_DK],
                          v_ref[0, rows, hd * M_DV:(hd + 1) * M_DV],
                          g[:, hd:hd + 1], g[:, M_HEADS + hd:M_HEADS + hd + 1],
                          C_ref[0, hd], n_ref[0, hd], m_ref[0, hd, :, 0:1]))
        for hd, (h, C_new, n_new, m_new) in enumerate(_mlstm_chunks(units)):
            hs_ref[0, rows, hd * M_DV:(hd + 1) * M_DV] = h.astype(hs_ref.dtype)
            C_ref[0, hd] = C_new
            n_ref[0, hd] = n_new
            m_ref[0, hd] = jnp.broadcast_to(m_new, (1, LANES))
        return carry

    lax.fori_loop(0, T // chunk, body, 0)


def _mlstm_prompt(proj, gates, B, T):
    chunk = MLSTM_CHUNK
    assert T % chunk == 0
    return pl.pallas_call(
        functools.partial(_mlstm_prompt_kernel, chunk=chunk),
        grid=(B,),
        in_specs=[
            pl.BlockSpec((1, T, M_HK), lambda b: (b, 0, 0)),
            pl.BlockSpec((1, T, M_HK), lambda b: (b, 0, 1)),
            pl.BlockSpec((1, T, M_HV), lambda b: (b, 0, 2 * M_HK // M_HV)),
            pl.BlockSpec((1, T, LANES), lambda b: (b, 0, 0)),
        ],
        out_specs=[
            pl.BlockSpec((1, T, M_HV), lambda b: (b, 0, 0)),
            pl.BlockSpec((1, M_HEADS, M_DK, M_DV), lambda b: (b, 0, 0, 0)),
            pl.BlockSpec((1, M_HEADS, 1, M_DK), lambda b: (b, 0, 0, 0)),
            pl.BlockSpec((1, M_HEADS, 1, LANES), lambda b: (b, 0, 0, 0)),
        ],
        out_shape=[
            jax.ShapeDtypeStruct((B, T, M_HV), BF16),
            jax.ShapeDtypeStruct((B, M_HEADS, M_DK, M_DV), F32),
            jax.ShapeDtypeStruct((B, M_HEADS, 1, M_DK), F32),
            jax.ShapeDtypeStruct((B, M_HEADS, 1, LANES), F32),
        ],
        compiler_params=_params("parallel"),
        name="mlstm_prompt",
    )(proj, proj, proj, gates)


def _mlstm_sample_kernel(q_ref, k_ref, v_ref, g_ref, C0_ref, n0_ref, m0_ref,
                         hs_ref, C_ref, n_ref, m_ref, *, T):
    nb = C0_ref.shape[0]

    def body(i, carry):
        units, where = [], []
        for s in range(SAMPLE_UNROLL):
            b = i * SAMPLE_UNROLL + s
            rows = pl.ds(pl.multiple_of(b * T, T), T)
            g = g_ref[rows, :]
            for hd in range(M_HEADS):
                units.append((q_ref[rows, hd * M_DK:(hd + 1) * M_DK].astype(BF16),
                              k_ref[rows, hd * M_DK:(hd + 1) * M_DK].astype(BF16),
                              v_ref[rows, hd * M_DV:(hd + 1) * M_DV].astype(BF16),
                              g[:, hd:hd + 1], g[:, M_HEADS + hd:M_HEADS + hd + 1],
                              C0_ref[b, hd], n0_ref[b, hd], m0_ref[b, hd, :, 0:1]))
                where.append((b, rows, hd))
        for (b, rows, hd), (h, C_new, n_new, m_new) in zip(where, _mlstm_chunks(units)):
            hs_ref[rows, hd * M_DV:(hd + 1) * M_DV] = h.astype(hs_ref.dtype)
            C_ref[b, hd] = C_new
            n_ref[b, hd] = n_new
            m_ref[b, hd] = jnp.broadcast_to(m_new, (1, LANES))
        return carry

    lax.fori_loop(0, nb // SAMPLE_UNROLL, body, 0)


def _mlstm_sample(proj, gates, C0, layer, n0, m0, B, T):
    nb = SAMPLE_BLOCK
    assert B % nb == 0 and T % SUBLANES == 0
    rows = nb * T
    return pl.pallas_call(
        functools.partial(_mlstm_sample_kernel, T=T),
        grid=(B // nb,),
        in_specs=[
            pl.BlockSpec((rows, M_HK), lambda i: (i, 0)),
            pl.BlockSpec((rows, M_HK), lambda i: (i, 1)),
            pl.BlockSpec((rows, M_HV), lambda i: (i, 2 * M_HK // M_HV)),
            pl.BlockSpec((rows, LANES), lambda i: (i, 0)),
            pl.BlockSpec((None, nb, M_HEADS, M_DK, M_DV), lambda i: (layer, i, 0, 0, 0)),
            pl.BlockSpec((nb, M_HEADS, 1, M_DK), lambda i: (i, 0, 0, 0)),
            pl.BlockSpec((nb, M_HEADS, 1, LANES), lambda i: (i, 0, 0, 0)),
        ],
        out_specs=[
            pl.BlockSpec((rows, M_HV), lambda i: (i, 0)),
            pl.BlockSpec((nb, M_HEADS, M_DK, M_DV), lambda i: (i, 0, 0, 0)),
            pl.BlockSpec((nb, M_HEADS, 1, M_DK), lambda i: (i, 0, 0, 0)),
            pl.BlockSpec((nb, M_HEADS, 1, LANES), lambda i: (i, 0, 0, 0)),
        ],
        out_shape=[
            jax.ShapeDtypeStruct((B * T, M_HV), BF16),
            jax.ShapeDtypeStruct((B, M_HEADS, M_DK, M_DV), F32),
            jax.ShapeDtypeStruct((B, M_HEADS, 1, M_DK), F32),
            jax.ShapeDtypeStruct((B, M_HEADS, 1, LANES), F32),
        ],
        compiler_params=_params("parallel"),
        name="mlstm_sample",
    )(proj, proj, proj, gates, C0, n0, m0)


def _mlstm_out_kernel(hs_ref, o_ref, hw_ref, w_ref, x_ref, y_ref):
    hs = hs_ref[...].astype(F32)
    parts = []
    for hd in range(M_HEADS):
        seg = hs[:, hd * M_DV:(hd + 1) * M_DV]
        ms = jnp.mean(seg * seg, axis=-1, keepdims=True)
        parts.append(seg * lax.rsqrt(ms + EPS))
    hn = jnp.concatenate(parts, axis=1) * hw_ref[...]
    a = hn * jax.nn.sigmoid(o_ref[...].astype(F32))
    y_ref[...] = x_ref[...] + _dot(a.astype(BF16), w_ref[...])


def _mlstm_out(hs, proj, hw, w, x):
    n = x.shape[0]
    tm = TOKEN_TILE
    o_block = (2 * M_HK + M_HV) // M_HV
    return pl.pallas_call(
        _mlstm_out_kernel,
        grid=(n // tm,),
        in_specs=[
            pl.BlockSpec((tm, M_HV), lambda i: (i, 0)),
            pl.BlockSpec((tm, M_HV), lambda i: (i, o_block)),
            _resident((1, M_HV)),
            _resident((M_HV, D_MODEL)),
            pl.BlockSpec((tm, D_MODEL), lambda i: (i, 0)),
        ],
        out_specs=pl.BlockSpec((tm, D_MODEL), lambda i: (i, 0)),
        out_shape=jax.ShapeDtypeStruct((n, D_MODEL), F32),
        compiler_params=_params("parallel"),
        name="mlstm_out",
    )(hs, proj, hw, w, x)


def _head_norm_rope(y, w, cos, s_lo, s_hi):
    lane = lax.broadcasted_iota(jnp.int32, y.shape, 1)
    first = lane < S_HD
    sq = y * y
    lo = jnp.sum(jnp.where(first, sq, 0.0), axis=-1, keepdims=True)
    hi = jnp.sum(jnp.where(first, 0.0, sq), axis=-1, keepdims=True)
    ms = jnp.where(first, lo, hi) * (1.0 / S_HD)
    z = y * lax.rsqrt(ms + EPS) * w
    half = ROT_DIM // 2
    return z * cos + pltpu.roll(z, LANES - half, 1) * s_lo + pltpu.roll(z, half, 1) * s_hi


def _swa_qkv_kernel(x_ref, nw_ref, w_ref, wvt_ref, qw_ref, kw_ref, cos_ref, slo_ref, shi_ref,
                    q_ref, k_ref, v_ref, vt_ref):
    hb = _rmsnorm(x_ref[...], nw_ref[...]).astype(BF16)
    cos, s_lo, s_hi = cos_ref[...], slo_ref[...], shi_ref[...]
    for j in range(S_HQ // LANES):
        cols = slice(j * LANES, (j + 1) * LANES)
        y = _head_norm_rope(_dot(hb, w_ref[:, cols]), qw_ref[...], cos, s_lo, s_hi)
        q_ref[:, cols] = (y * (S_HD ** -0.5)).astype(q_ref.dtype)
    for j in range(S_HKV // LANES):
        cols = slice(j * LANES, (j + 1) * LANES)
        wcols = slice(S_HQ + j * LANES, S_HQ + (j + 1) * LANES)
        k_ref[:, cols] = _head_norm_rope(_dot(hb, w_ref[:, wcols]), kw_ref[...], cos, s_lo, s_hi)
    v_ref[...] = _dot(hb, w_ref[:, S_HQ + S_HKV:])
    vt_ref[...] = _dot_nt(wvt_ref[...], hb).astype(vt_ref.dtype)


def _swa_qkv(x, nw, w, wvt, qw, kw, cos, s_lo, s_hi):
    n = x.shape[0]
    tm = TOKEN_TILE
    tab_blocks = cos.shape[0] // tm
    tab = pl.BlockSpec((tm, LANES), lambda i: (i % tab_blocks, 0))
    return pl.pallas_call(
        _swa_qkv_kernel,
        grid=(n // tm,),
        in_specs=[
            pl.BlockSpec((tm, D_MODEL), lambda i: (i, 0)),
            _resident((1, D_MODEL)),
            _resident((D_MODEL, S_HQ + 2 * S_HKV)),
            _resident((S_HKV, D_MODEL)),
            _resident((1, LANES)),
            _resident((1, LANES)),
            tab, tab, tab,
        ],
        out_specs=[
            pl.BlockSpec((tm, S_HQ), lambda i: (i, 0)),
            pl.BlockSpec((tm, S_HKV), lambda i: (i, 0)),
            pl.BlockSpec((tm, S_HKV), lambda i: (i, 0)),
            pl.BlockSpec((S_HKV, tm), lambda i: (0, i)),
        ],
        out_shape=[
            jax.ShapeDtypeStruct((n, S_HQ), BF16),
            jax.ShapeDtypeStruct((n, S_HKV), F32),
            jax.ShapeDtypeStruct((n, S_HKV), F32),
            jax.ShapeDtypeStruct((S_HKV, n), BF16),
        ],
        compiler_params=_params("parallel"),
        name="swa_qkv",
    )(x, nw, w, wvt, qw, kw, cos, s_lo, s_hi)


def _rope_tables(pos):
    half = ROT_DIM // 2
    inv = ROPE_THETA ** (-jnp.arange(half, dtype=F32) * 2.0 / ROT_DIM)
    ang = pos[:, None] * inv[None, :]
    cos, sin = jnp.cos(ang), jnp.sin(ang)
    n = pos.shape[0]
    pad = jnp.zeros((n, S_HD - ROT_DIM), F32)
    zero = jnp.zeros((n, half), F32)
    cos_h = jnp.concatenate([cos, cos, pad + 1.0], axis=1)
    lo_h = jnp.concatenate([-sin, zero, pad], axis=1)
    hi_h = jnp.concatenate([zero, sin, pad], axis=1)
    two = lambda t: jnp.concatenate([t, t], axis=1)
    return two(cos_h), two(lo_h), two(hi_h)


def _softmax_over_keys(s, mask, sink):
    s = jnp.where(mask, s, NEG_INF)
    m = jnp.maximum(jnp.max(s, axis=0, keepdims=True), sink)
    p = jnp.exp(s - m)
    denom = jnp.sum(p, axis=0, keepdims=True) + jnp.exp(sink - m)
    return p * (1.0 / denom)


def _swa_prompt_kernel(sink_ref, q_ref, kp_ref, ko_ref, vtp_ref, vto_ref, o_ref):
    blk = pl.program_id(1)
    W = WINDOW
    kf = jnp.concatenate([kp_ref[0], ko_ref[0]], axis=0)
    vt = jnp.concatenate([vtp_ref[...], vto_ref[...]], axis=1)
    j = lax.broadcasted_iota(jnp.int32, (2 * W, W), 0)
    rel = j - lax.broadcasted_iota(jnp.int32, (2 * W, W), 1)
    mask = (rel >= 0) & (rel <= W) & ((j >= W) | (blk > 0))
    low = lax.broadcasted_iota(jnp.int32, (2 * W, LANES), 1) < S_HD
    q = q_ref[0]
    halves = LANES // S_HD
    k_padded = {}
    for c in range(S_HKV // LANES):
        kc = kf[:, c * LANES:(c + 1) * LANES]
        kr = pltpu.roll(kc, S_HD, 1)
        for half in range(halves):
            k_lo = jnp.where(low, kc if half == 0 else kr, 0.0).astype(BF16)
            k_hi = jnp.where(low, 0.0, kr if half == 0 else kc).astype(BF16)
            k_padded[c * halves + half] = (k_lo, k_hi)

    def scores(h):
        pair = h // halves
        return _dot_nt(k_padded[h // S_GROUP][h % halves], q[:, pair * LANES:(pair + 1) * LANES])

    pending = [scores(h) for h in range(SCORE_LOOKAHEAD)]
    outs = []
    for h in range(S_HEADS):
        g = h // S_GROUP
        sink = sink_ref[h]
        s = jnp.where(mask, pending.pop(0), NEG_INF)
        m = jnp.maximum(jnp.max(s, axis=0, keepdims=True), sink)
        p = jnp.exp(s - m)
        denom = jnp.sum(p, axis=0, keepdims=True) + jnp.exp(sink - m)
        if h + SCORE_LOOKAHEAD < S_HEADS:
            pending.append(scores(h + SCORE_LOOKAHEAD))
        outs.append(_dot(vt[g * S_HD:(g + 1) * S_HD, :], p.astype(BF16)) * (1.0 / denom))
        if len(outs) == halves:
            pair = h // halves
            o_pair = jnp.concatenate(outs, axis=0).T
            o_ref[0, :, pair * LANES:(pair + 1) * LANES] = o_pair.astype(o_ref.dtype)
            outs = []


def _swa_prompt(q, k, vt, sinks, B, T):
    W = WINDOW
    nb = T // W
    prev = lambda b, i: (b, jnp.maximum(i - 1, 0), 0)
    own = lambda b, i: (b, i, 0)
    return pl.pallas_call(
        _swa_prompt_kernel,
        grid=(B, nb),
        in_specs=[
            pl.BlockSpec(memory_space=pltpu.SMEM),
            pl.BlockSpec((1, W, S_HQ), own),
            pl.BlockSpec((1, W, S_HKV), prev),
            pl.BlockSpec((1, W, S_HKV), own),
            pl.BlockSpec((S_HKV, W), lambda b, i: (0, b * nb + jnp.maximum(i - 1, 0))),
            pl.BlockSpec((S_HKV, W), lambda b, i: (0, b * nb + i)),
        ],
        out_specs=pl.BlockSpec((1, W, S_HQ), own),
        out_shape=jax.ShapeDtypeStruct((B, T, S_HQ), BF16),
        compiler_params=_params("parallel", "parallel"),
        name="swa_prompt",
    )(sinks, q, k, k, vt, vt)


SAMPLE_KEY_PAD = 16


def _swa_sample_kernel(q_ref, kn_ref, vn_ref, kc_ref, vc_ref, sink_ref, o_ref, ko_ref, vo_ref, *, T):
    nb, W = kc_ref.shape[0], kc_ref.shape[1]
    J = W + SAMPLE_KEY_PAD
    j = lax.broadcasted_iota(jnp.int32, (J, LANES), 0)
    t = lax.rem(lax.broadcasted_iota(jnp.int32, (J, LANES), 1), T)
    rel = t + W - j
    mask = (rel >= 0) & (rel <= WINDOW)
    low = lax.broadcasted_iota(jnp.int32, (T, LANES), 1) < S_HD
    zero = jnp.zeros((T, LANES), F32)
    kpad = jnp.zeros((SAMPLE_KEY_PAD - T, S_HKV), F32)
    sink = sink_ref[...]
    halves = LANES // S_HD
    for b in range(nb):
        rows = slice(b * T, (b + 1) * T)
        kn, vn, kc, vc = kn_ref[rows, :], vn_ref[rows, :], kc_ref[b], vc_ref[b]
        ko_ref[b, :W - T] = kc[T:]
        ko_ref[b, W - T:] = kn
        vo_ref[b, :W - T] = vc[T:]
        vo_ref[b, W - T:] = vn
        kall = jnp.concatenate([kc, kn, kpad], axis=0).astype(BF16)
        vall = jnp.concatenate([vc, vn, kpad], axis=0).astype(BF16)
        qf = q_ref[rows, :].astype(F32)
        blocks = []
        for h in range(S_HEADS):
            g = h // S_GROUP
            src = qf[:, (h // halves) * LANES:(h // halves + 1) * LANES]
            if h % halves != g % halves:
                src = pltpu.roll(src, S_HD, 1)
            src = jnp.where(low if g % halves == 0 else ~low, src, 0.0)
            blocks.append(jnp.concatenate(
                [src if c == g // halves else zero for c in range(S_HKV // LANES)], axis=1))
        qblk = jnp.concatenate(blocks, axis=0).astype(BF16)
        pn = _softmax_over_keys(_dot_nt(kall, qblk), mask, sink)
        o_full = _dot_tn(pn.astype(BF16), vall)
        for pair in range(S_HEADS // halves):
            pieces = []
            for hh in range(halves):
                h = pair * halves + hh
                g = h // S_GROUP
                piece = o_full[h * T:(h + 1) * T, (g // halves) * LANES:(g // halves + 1) * LANES]
                if g % halves != hh:
                    piece = pltpu.roll(piece, S_HD, 1)
                pieces.append(piece)
            o_ref[rows, pair * LANES:(pair + 1) * LANES] = jnp.where(low, pieces[0], pieces[1]).astype(o_ref.dtype)


def _swa_sample(q, kn, vn, kc, vc, layer, sinks, B, T):
    nb = SAMPLE_BLOCK
    W = kc.shape[2]
    rows = nb * T
    assert S_HEADS * T == LANES and LANES // S_HD == 2 and T <= SAMPLE_KEY_PAD
    return pl.pallas_call(
        functools.partial(_swa_sample_kernel, T=T),
        grid=(B // nb,),
        in_specs=[
            pl.BlockSpec((rows, S_HQ), lambda i: (i, 0)),
            pl.BlockSpec((rows, S_HKV), lambda i: (i, 0)),
            pl.BlockSpec((rows, S_HKV), lambda i: (i, 0)),
            pl.BlockSpec((None, nb, W, S_HKV), lambda i: (layer, i, 0, 0)),
            pl.BlockSpec((None, nb, W, S_HKV), lambda i: (layer, i, 0, 0)),
            _resident((1, LANES)),
        ],
        out_specs=[
            pl.BlockSpec((rows, S_HQ), lambda i: (i, 0)),
            pl.BlockSpec((nb, W, S_HKV), lambda i: (i, 0, 0)),
            pl.BlockSpec((nb, W, S_HKV), lambda i: (i, 0, 0)),
        ],
        out_shape=[
            jax.ShapeDtypeStruct((B * T, S_HQ), BF16),
            jax.ShapeDtypeStruct((B, W, S_HKV), F32),
            jax.ShapeDtypeStruct((B, W, S_HKV), F32),
        ],
        compiler_params=_params("parallel"),
        name="swa_sample",
    )(q, kn, vn, kc, vc, sinks)


def _proj_residual_kernel(a_ref, w_ref, x_ref, y_ref):
    y_ref[...] = x_ref[...] + _dot(a_ref[...], w_ref[...])


def _proj_residual(a, w, x):
    n = x.shape[0]
    tm = TOKEN_TILE
    return pl.pallas_call(
        _proj_residual_kernel,
        grid=(n // tm,),
        in_specs=[
            pl.BlockSpec((tm, a.shape[1]), lambda i: (i, 0)),
            _resident(w.shape),
            pl.BlockSpec((tm, D_MODEL), lambda i: (i, 0)),
        ],
        out_specs=pl.BlockSpec((tm, D_MODEL), lambda i: (i, 0)),
        out_shape=jax.ShapeDtypeStruct((n, D_MODEL), F32),
        compiler_params=_params("parallel"),
        name="proj_residual",
    )(a, w, x)


def _gelu(x):
    return 0.5 * x * (1.0 + lax.erf(x * (2.0 ** -0.5)))


def _convglu_body(x_ref, nw_ref, wg_ref, wu_ref, cw_ref, cb_ref, wd_ref, y_ref, act_ref, prev_rows,
                  mixer=None):
    y_ref[...] = x_ref[...] if mixer is None else x_ref[...] + _dot(mixer[0][...], mixer[1][...])
    x_ref = y_ref
    hb = _rmsnorm(x_ref[...], nw_ref[...]).astype(BF16)
    for f in range(D_FF // FF_CHUNK):
        cols = slice(f * FF_CHUNK, (f + 1) * FF_CHUNK)
        g = _dot(hb, wg_ref[:, cols])
        u = _dot(hb, wu_ref[:, cols])
        g1, g2 = prev_rows(cols, g)
        gc = cb_ref[:, cols] + cw_ref[0:1, cols] * g2 + cw_ref[1:2, cols] * g1 + cw_ref[2:3, cols] * g
        act_ref[:, cols] = (_gelu(gc) * u).astype(BF16)
    y_ref[...] = x_ref[...] + _dot(act_ref[...], wd_ref[...])


def _convglu_prompt_kernel(*refs, tiles_per_seq, fused_mixer):
    mixer = tuple(refs[:2]) if fused_mixer else None
    (x_ref, nw_ref, wg_ref, wu_ref, cw_ref, cb_ref, wd_ref,
     y_ref, g_ref, act_ref, tail_ref) = refs[2:] if fused_mixer else refs
    tm = x_ref.shape[0]
    row = lax.broadcasted_iota(jnp.int32, (tm, FF_CHUNK), 0)

    @pl.when(pl.program_id(0) % tiles_per_seq == 0)
    def _():
        tail_ref[...] = jnp.zeros_like(tail_ref)

    def prev_rows(cols, g):
        tail = tail_ref[:, cols]
        last, last2 = tail[SUBLANES - 1:SUBLANES, :], tail[SUBLANES - 2:SUBLANES - 1, :]
        g1 = jnp.where(row < 1, last, pltpu.roll(g, 1, 0))
        g2 = jnp.where(row < 2, jnp.where(row < 1, last2, last), pltpu.roll(g, 2, 0))
        tail_ref[:, cols] = g[tm - SUBLANES:, :]
        return g1, g2

    _convglu_body(x_ref, nw_ref, wg_ref, wu_ref, cw_ref, cb_ref, wd_ref, y_ref, act_ref, prev_rows, mixer)
    g_ref[0] = tail_ref[...]


def _convglu_sample_kernel(x_ref, nw_ref, wg_ref, wu_ref, cw_ref, cb_ref, wd_ref, e_ref,
                           y_ref, g_ref, act_ref, *, seq_len):
    tm = x_ref.shape[0]
    tpos = lax.broadcasted_iota(jnp.int32, (tm, FF_CHUNK), 0) % seq_len

    def prev_rows(cols, g):
        g_ref[:, cols] = g
        e = e_ref[:, cols]
        g1 = jnp.where(tpos < 1, pltpu.roll(e, tm - 1, 0), pltpu.roll(g, 1, 0))
        g2 = jnp.where(tpos < 2, e, pltpu.roll(g, 2, 0))
        return g1, g2

    _convglu_body(x_ref, nw_ref, wg_ref, wu_ref, cw_ref, cb_ref, wd_ref, y_ref, act_ref, prev_rows)


def _convglu_prompt(x, nw, wg, wu, cw, cb, wd, B, T, mixer=None):
    n = x.shape[0]
    tm = TOKEN_TILE
    assert T % tm == 0
    tiles_per_seq = T // tm
    mixer_specs = [] if mixer is None else [
        pl.BlockSpec((tm, mixer[0].shape[1]), lambda i: (i, 0)), _resident(mixer[1].shape)]

    return pl.pallas_call(
        functools.partial(_convglu_prompt_kernel, tiles_per_seq=tiles_per_seq, fused_mixer=mixer is not None),
        grid=(n // tm,),
        in_specs=mixer_specs + [
            pl.BlockSpec((tm, D_MODEL), lambda i: (i, 0)),
            _resident((1, D_MODEL)),
            _resident((D_MODEL, D_FF)),
            _resident((D_MODEL, D_FF)),
            _resident((CONV_W, D_FF)),
            _resident((1, D_FF)),
            _resident((D_FF, D_MODEL)),
        ],
        out_specs=[
            pl.BlockSpec((tm, D_MODEL), lambda i: (i, 0)),
            pl.BlockSpec((1, SUBLANES, D_FF), lambda i: (i // tiles_per_seq, 0, 0)),
        ],
        out_shape=[
            jax.ShapeDtypeStruct((n, D_MODEL), F32),
            jax.ShapeDtypeStruct((B, SUBLANES, D_FF), F32),
        ],
        scratch_shapes=[pltpu.VMEM((tm, D_FF), BF16), pltpu.VMEM((SUBLANES, D_FF), F32)],
        compiler_params=_params("arbitrary"),
        name="convglu_prompt",
    )(*(mixer or ()), x, nw, wg, wu, cw, cb, wd)


def _convglu_sample(x, nw, wg, wu, cw, cb, wd, e, T):
    n = x.shape[0]
    tm = TOKEN_TILE
    assert tm % T == 0 and T >= CONV_W - 1 and CONV_W == 3

    return pl.pallas_call(
        functools.partial(_convglu_sample_kernel, seq_len=T),
        grid=(n // tm,),
        in_specs=[
            pl.BlockSpec((tm, D_MODEL), lambda i: (i, 0)),
            _resident((1, D_MODEL)),
            _resident((D_MODEL, D_FF)),
            _resident((D_MODEL, D_FF)),
            _resident((CONV_W, D_FF)),
            _resident((1, D_FF)),
            _resident((D_FF, D_MODEL)),
            pl.BlockSpec((tm, D_FF), lambda i: (i, 0)),
        ],
        out_specs=[
            pl.BlockSpec((tm, D_MODEL), lambda i: (i, 0)),
            pl.BlockSpec((tm, D_FF), lambda i: (i, 0)),
        ],
        out_shape=[
            jax.ShapeDtypeStruct((n, D_MODEL), F32),
            jax.ShapeDtypeStruct((n, D_FF), F32),
        ],
        scratch_shapes=[pltpu.VMEM((tm, D_FF), BF16)],
        compiler_params=_params("parallel"),
        name="convglu_sample",
    )(x, nw, wg, wu, cw, cb, wd, e)


def kernel(x_prompt, x_sample, state_mlstm_C, state_mlstm_n, state_mlstm_m, cache_swa_k, cache_swa_v,
           state_ffn_conv, norm_mix, norm_ffn, w_mlstm_in, b_mlstm_gates, mlstm_head_norm, w_mlstm_out,
           w_swa_qkv, swa_q_norm, swa_k_norm, swa_sinks, w_swa_out, w_ffn_up, ffn_conv_w, ffn_conv_b,
           w_ffn_down):
    B, T, _ = x_prompt.shape
    Bs, Ts, _ = x_sample.shape
    W = cache_swa_k.shape[2]
    xp = x_prompt.reshape(B * T, D_MODEL)
    xs = x_sample.reshape(Bs * Ts, D_MODEL)
    n_gates = 2 * M_HEADS
    cache_k = cache_swa_k.reshape(cache_swa_k.shape[0], Bs, W, S_HKV)
    cache_v = cache_swa_v.reshape(cache_swa_v.shape[0], Bs, W, S_HKV)

    tab_p = _rope_tables(jnp.arange(T, dtype=F32))
    tab_s = tuple(jnp.tile(t, (TOKEN_TILE // Ts, 1))
                  for t in _rope_tables(PAST_LEN + jnp.arange(Ts, dtype=F32)))

    pC, pn, pm, pk, pv, pconv = [], [], [], [], [], []
    sC, sn, sm, sk, sv, sconv = [], [], [], [], [], []
    for i in range(DEPTH):
        j = i // 2
        nw = norm_mix[i][None]
        mixer_p = None
        if i % 2 == 0:
            w_in = jnp.pad(w_mlstm_in[j], ((0, 0), (0, LANES - n_gates))).astype(BF16)
            bg = jnp.pad(b_mlstm_gates[j], (0, LANES - n_gates))[None]
            hw = mlstm_head_norm[j][None]
            wo = w_mlstm_out[j].astype(BF16)

            proj, gates = _mlstm_inproj(xp, nw, w_in, bg, BF16)
            hs, C1, n1, m1 = _mlstm_prompt(proj.reshape(B, T, M_PROJ), gates.reshape(B, T, LANES), B, T)
            xp = _mlstm_out(hs.reshape(B * T, M_HV), proj, hw, wo, xp)
            pC.append(C1)
            pn.append(n1[:, :, 0])
            pm.append(m1[:, :, 0, 0])

            proj, gates = _mlstm_inproj(xs, nw, w_in, bg, F32)
            m0 = jnp.broadcast_to(state_mlstm_m[j][:, :, None, None], (Bs, M_HEADS, 1, LANES))
            hs, C2, n2, m2 = _mlstm_sample(proj, gates, state_mlstm_C, j, state_mlstm_n[j][:, :, None, :],
                                           m0, Bs, Ts)
            xs = _mlstm_out(hs, proj, hw, wo, xs)
            sC.append(C2)
            sn.append(n2[:, :, 0])
            sm.append(m2[:, :, 0, 0])
        else:
            w = w_swa_qkv[j].astype(BF16)
            qw = jnp.tile(swa_q_norm[j], LANES // S_HD)[None]
            kw = jnp.tile(swa_k_norm[j], LANES // S_HD)[None]
            wvt = w[:, S_HQ + S_HKV:].T
            sinks = swa_sinks[j]
            wo = w_swa_out[j].astype(BF16)

            q, k, v, vt = _swa_qkv(xp, nw, w, wvt, qw, kw, *tab_p)
            o = _swa_prompt(q.reshape(B, T, S_HQ), k.reshape(B, T, S_HKV), vt, sinks, B, T)
            mixer_p = (o.reshape(B * T, S_HQ), wo)
            pk.append(k.reshape(B, T, S_KV, S_HD)[:, T - WINDOW:])
            pv.append(v.reshape(B, T, S_KV, S_HD)[:, T - WINDOW:])

            q, k, v, _ = _swa_qkv(xs, nw, w, wvt, qw, kw, *tab_s)
            o, k2, v2 = _swa_sample(q, k, v, cache_k, cache_v, j, jnp.repeat(sinks, Ts)[None], Bs, Ts)
            xs = _proj_residual(o, wo, xs)
            sk.append(k2.reshape(Bs, W, S_KV, S_HD))
            sv.append(v2.reshape(Bs, W, S_KV, S_HD))

        nwf = norm_ffn[i][None]
        wg = w_ffn_up[i][:, :D_FF].astype(BF16)
        wu = w_ffn_up[i][:, D_FF:].astype(BF16)
        cw = ffn_conv_w[i]
        cb = ffn_conv_b[i][None]
        wd = w_ffn_down[i].astype(BF16)

        xp, gtail = _convglu_prompt(xp, nwf, wg, wu, cw, cb, wd, B, T, mixer_p)
        pconv.append(gtail[:, SUBLANES - (CONV_W - 1):])

        st = state_ffn_conv[i]
        e = jnp.pad(st, ((0, 0), (0, Ts - (CONV_W - 1)), (0, 0))).reshape(Bs * Ts, D_FF)
        xs, g = _convglu_sample(xs, nwf, wg, wu, cw, cb, wd, e, Ts)
        sconv.append(g.reshape(Bs, Ts, D_FF)[:, Ts - (CONV_W - 1):])

    return (xp.reshape(B, T, D_MODEL), xs.reshape(Bs, Ts, D_MODEL),
            jnp.stack(pC), jnp.stack(pn), jnp.stack(pm), jnp.stack(pk), jnp.stack(pv), jnp.stack(pconv),
            jnp.stack(sC), jnp.stack(sn), jnp.stack(sm), jnp.stack(sk), jnp.stack(sv), jnp.stack(sconv))
```

```python
import functools
import math

import jax
import jax.numpy as jnp
from jax import lax
from jax.experimental import pallas as pl
from jax.experimental.pallas import tpu as pltpu

F32 = jnp.float32
BF16 = jnp.bfloat16

D_MODEL = 1024
DEPTH = 4
PAST_LEN = 8192
M_HEADS = 4
M_DV = D_MODEL // M_HEADS
M_DK = M_DV // 2
M_HK = M_HEADS * M_DK
M_HV = M_HEADS * M_DV
M_PROJ = 2 * M_HK + 2 * M_HV
S_HD = 64
S_HEADS = D_MODEL // S_HD
S_KV = 4
S_GROUP = S_HEADS // S_KV
S_HQ = S_HEADS * S_HD
S_HKV = S_KV * S_HD
WINDOW = 128
ROT_DIM = S_HD // 4
ROPE_THETA = 500000.0
D_FF = 2816
CONV_W = 3
EPS = 1e-6

LANES = 128
SUBLANES = 8
VMEM_LIMIT = 56 * 1024 * 1024
TOKEN_TILE = 512
FF_CHUNK = 256
PROJ_CHUNK = 512
MLSTM_CHUNK = 256
SAMPLE_BLOCK = 8
SAMPLE_UNROLL = 4
SCORE_LOOKAHEAD = 16
NEG_INF = float("-inf")


def _params(*sem):
    return pltpu.CompilerParams(dimension_semantics=sem, vmem_limit_bytes=VMEM_LIMIT)


def _resident(shape):
    zeros = (0,) * len(shape)
    return pl.BlockSpec(shape, lambda *_: zeros, pipeline_mode=pl.Buffered(1))


def _rmsnorm(x, w):
    ms = jnp.mean(x * x, axis=-1, keepdims=True)
    return x * lax.rsqrt(ms + EPS) * w


def _dot(a, b):
    return jnp.dot(a, b, preferred_element_type=F32)


def _dot_nt(a, b):
    return lax.dot_general(a, b, (((1,), (1,)), ((), ())), preferred_element_type=F32)


def _dot_tn(a, b):
    return lax.dot_general(a, b, (((0,), (0,)), ((), ())), preferred_element_type=F32)


def _mlstm_inproj_kernel(x_ref, nw_ref, w_ref, bg_ref, proj_ref, gate_ref):
    hb = _rmsnorm(x_ref[...], nw_ref[...]).astype(BF16)
    for j in range(M_PROJ // PROJ_CHUNK):
        cols = slice(j * PROJ_CHUNK, (j + 1) * PROJ_CHUNK)
        r = _dot(hb, w_ref[:, cols])
        if j * PROJ_CHUNK == M_HK:
            r = r * (M_DK ** -0.5)
        proj_ref[:, cols] = r.astype(proj_ref.dtype)
    g = _dot(hb, w_ref[:, M_PROJ:]) + bg_ref[...]
    logf = -(jnp.maximum(-g, 0.0) + jnp.log1p(jnp.exp(-jnp.abs(g))))
    lane = lax.broadcasted_iota(jnp.int32, g.shape, 1)
    gate_ref[...] = jnp.where(lane < M_HEADS, g, logf)


def _mlstm_inproj(x, nw, w, bg, proj_dtype):
    n = x.shape[0]
    tm = TOKEN_TILE
    assert n % tm == 0 and M_HK == PROJ_CHUNK
    return pl.pallas_call(
        _mlstm_inproj_kernel,
        grid=(n // tm,),
        in_specs=[
            pl.BlockSpec((tm, D_MODEL), lambda i: (i, 0)),
            _resident((1, D_MODEL)),
            _resident((D_MODEL, M_PROJ + LANES)),
            _resident((1, LANES)),
        ],
        out_specs=[
            pl.BlockSpec((tm, M_PROJ), lambda i: (i, 0)),
            pl.BlockSpec((tm, LANES), lambda i: (i, 0)),
        ],
        out_shape=[
            jax.ShapeDtypeStruct((n, M_PROJ), proj_dtype),
            jax.ShapeDtypeStruct((n, LANES), F32),
        ],
        compiler_params=_params("parallel"),
        name="mlstm_inproj",
    )(x, nw, w, bg)


def _mlstm_chunks(units):
    L = units[0][0].shape[0]
    t_idx = lax.broadcasted_iota(jnp.int32, (L, L), 0)
    s_idx = lax.broadcasted_iota(jnp.int32, (L, L), 1)
    causal = s_idx <= t_idx
    eye = s_idx == t_idx

    def to_row(col):
        return jnp.sum(jnp.where(eye, col, 0.0), axis=0, keepdims=True)

    first = [(_dot_nt(q, k), _dot(q, C.astype(BF16))) for q, k, v, _, _, C, _, _ in units]
    mid = []
    for (q, k, v, i_col, f_col, C, n, m), (qk, qC) in zip(units, first):
        f_row = to_row(f_col)
        i_row = to_row(i_col)
        b_col = jnp.sum(jnp.where(causal, f_row, 0.0), axis=1, keepdims=True)
        b_row = to_row(b_col)
        d = jnp.where(causal, b_col - b_row + i_row, NEG_INF)
        m_inter = b_col + m
        m_t = jnp.maximum(m_inter, jnp.max(d, axis=1, keepdims=True))
        w_intra = jnp.exp(d - m_t) * qk
        w_inter = jnp.exp(m_inter - m_t)
        qn = jnp.sum(q.astype(F32) * n, axis=1, keepdims=True)
        nq = w_inter * qn + jnp.sum(w_intra, axis=1, keepdims=True)
        inv_den = 1.0 / jnp.maximum(jnp.abs(nq), jnp.exp(-m_t))
        m_new = m_t[L - 1:L, :]
        b_last = b_col[L - 1:L, :]
        w_state = jnp.exp(b_last - b_col + i_col - m_new)
        decay = jnp.exp(b_last + m - m_new)
        kw = k.astype(F32) * w_state
        n_new = decay * n + jnp.sum(kw, axis=0, keepdims=True)
        mid.append((w_intra.astype(BF16), kw.astype(BF16), w_inter, inv_den, decay, n_new, m_new))
    second = [(_dot(w_intra, u[2]), _dot_tn(kw, u[2])) for u, (w_intra, kw, *_) in zip(units, mid)]
    out = []
    for u, (_, qC), (_, _, w_inter, inv_den, decay, n_new, m_new), (wv, kv) in zip(units, first, mid, second):
        h = (w_inter * qC + wv) * inv_den
        out.append((h, decay * u[5] + kv, n_new, m_new))
    return out


def _mlstm_prompt_kernel(q_ref, k_ref, v_ref, g_ref, hs_ref, C_ref, n_ref, m_ref, *, chunk):
    T = q_ref.shape[1]
    C_ref[...] = jnp.zeros_like(C_ref)
    n_ref[...] = jnp.zeros_like(n_ref)
    m_ref[...] = jnp.zeros_like(m_ref)

    def body(c, carry):
        rows = pl.ds(pl.multiple_of(c * chunk, chunk), chunk)
        g = g_ref[0, rows, :]
        units = []
        for hd in range(M_HEADS):
            units.append((q_ref[0, rows, hd * M_DK:(hd + 1) * M_DK],
                          k_ref[0, rows, hd * M_DK:(hd + 1) * M_DK],
                          v_ref[0, rows, hd * M_DV:(hd + 1) * M_DV],
                          g[:, hd:hd + 1], g[:, M_HEADS + hd:M_HEADS + hd + 1],
                          C_ref[0, hd], n_ref[0, hd], m_ref[0, hd, :, 0:1]))
        for hd, (h, C_new, n_new, m_new) in enumerate(_mlstm_chunks(units)):
            hs_ref[0, rows, hd * M_DV:(hd + 1) * M_DV] = h.astype(hs_ref.dtype)
            C_ref[0, hd] = C_new
            n_ref[0, hd] = n_new
            m_ref[0, hd] = jnp.broadcast_to(m_new, (1, LANES))
        return carry

    lax.fori_loop(0, T // chunk, body, 0)


def _mlstm_prompt(proj, gates, B, T):
    chunk = MLSTM_CHUNK
    assert T % chunk == 0
    return pl.pallas_call(
        functools.partial(_mlstm_prompt_kernel, chunk=chunk),
        grid=(B,),
        in_specs=[
            pl.BlockSpec((1, T, M_HK), lambda b: (b, 0, 0)),
            pl.BlockSpec((1, T, M_HK), lambda b: (b, 0, 1)),
            pl.BlockSpec((1, T, M_HV), lambda b: (b, 0, 2 * M_HK // M_HV)),
            pl.BlockSpec((1, T, LANES), lambda b: (b, 0, 0)),
        ],
        out_specs=[
            pl.BlockSpec((1, T, M_HV), lambda b: (b, 0, 0)),
            pl.BlockSpec((1, M_HEADS, M_DK, M_DV), lambda b: (b, 0, 0, 0)),
            pl.BlockSpec((1, M_HEADS, 1, M_DK), lambda b: (b, 0, 0, 0)),
            pl.BlockSpec((1, M_HEADS, 1, LANES), lambda b: (b, 0, 0, 0)),
        ],
        out_shape=[
            jax.ShapeDtypeStruct((B, T, M_HV), BF16),
            jax.ShapeDtypeStruct((B, M_HEADS, M_DK, M_DV), F32),
            jax.ShapeDtypeStruct((B, M_HEADS, 1, M_DK), F32),
            jax.ShapeDtypeStruct((B, M_HEADS, 1, LANES), F32),
        ],
        compiler_params=_params("parallel"),
        name="mlstm_prompt",
    )(proj, proj, proj, gates)


def _mlstm_sample_kernel(q_ref, k_ref, v_ref, g_ref, C0_ref, n0_ref, m0_ref,
                         hs_ref, C_ref, n_ref, m_ref, *, T):
    nb = C0_ref.shape[0]

    def body(i, carry):
        units, where = [], []
        for s in range(SAMPLE_UNROLL):
            b = i * SAMPLE_UNROLL + s
            rows = pl.ds(pl.multiple_of(b * T, T), T)
            g = g_ref[rows, :]
            for hd in range(M_HEADS):
                units.append((q_ref[rows, hd * M_DK:(hd + 1) * M_DK].astype(BF16),
                              k_ref[rows, hd * M_DK:(hd + 1) * M_DK].astype(BF16),
                              v_ref[rows, hd * M_DV:(hd + 1) * M_DV].astype(BF16),
                              g[:, hd:hd + 1], g[:, M_HEADS + hd:M_HEADS + hd + 1],
                              C0_ref[b, hd], n0_ref[b, hd], m0_ref[b, hd, :, 0:1]))
                where.append((b, rows, hd))
        for (b, rows, hd), (h, C_new, n_new, m_new) in zip(where, _mlstm_chunks(units)):
            hs_ref[rows, hd * M_DV:(hd + 1) * M_DV] = h.astype(hs_ref.dtype)
            C_ref[b, hd] = C_new
            n_ref[b, hd] = n_new
            m_ref[b, hd] = jnp.broadcast_to(m_new, (1, LANES))
        return carry

    lax.fori_loop(0, nb // SAMPLE_UNROLL, body, 0)


def _mlstm_sample(proj, gates, C0, layer, n0, m0, B, T):
    nb = SAMPLE_BLOCK
    assert B % nb == 0 and T % SUBLANES == 0
    rows = nb * T
    return pl.pallas_call(
        functools.partial(_mlstm_sample_kernel, T=T),
        grid=(B // nb,),
        in_specs=[
            pl.BlockSpec((rows, M_HK), lambda i: (i, 0)),
            pl.BlockSpec((rows, M_HK), lambda i: (i, 1)),
            pl.BlockSpec((rows, M_HV), lambda i: (i, 2 * M_HK // M_HV)),
            pl.BlockSpec((rows, LANES), lambda i: (i, 0)),
            pl.BlockSpec((None, nb, M_HEADS, M_DK, M_DV), lambda i: (layer, i, 0, 0, 0)),
            pl.BlockSpec((nb, M_HEADS, 1, M_DK), lambda i: (i, 0, 0, 0)),
            pl.BlockSpec((nb, M_HEADS, 1, LANES), lambda i: (i, 0, 0, 0)),
        ],
        out_specs=[
            pl.BlockSpec((rows, M_HV), lambda i: (i, 0)),
            pl.BlockSpec((nb, M_HEADS, M_DK, M_DV), lambda i: (i, 0, 0, 0)),
            pl.BlockSpec((nb, M_HEADS, 1, M_DK), lambda i: (i, 0, 0, 0)),
            pl.BlockSpec((nb, M_HEADS, 1, LANES), lambda i: (i, 0, 0, 0)),
        ],
        out_shape=[
            jax.ShapeDtypeStruct((B * T, M_HV), BF16),
            jax.ShapeDtypeStruct((B, M_HEADS, M_DK, M_DV), F32),
            jax.ShapeDtypeStruct((B, M_HEADS, 1, M_DK), F32),
            jax.ShapeDtypeStruct((B, M_HEADS, 1, LANES), F32),
        ],
        compiler_params=_params("parallel"),
        name="mlstm_sample",
    )(proj, proj, proj, gates, C0, n0, m0)


def _mlstm_out_kernel(hs_ref, o_ref, hw_ref, w_ref, x_ref, y_ref):
    hs = hs_ref[...].astype(F32)
    parts = []
    for hd in range(M_HEADS):
        seg = hs[:, hd * M_DV:(hd + 1) * M_DV]
        ms = jnp.mean(seg * seg, axis=-1, keepdims=True)
        parts.append(seg * lax.rsqrt(ms + EPS))
    hn = jnp.concatenate(parts, axis=1) * hw_ref[...]
    a = hn * jax.nn.sigmoid(o_ref[...].astype(F32))
    y_ref[...] = x_ref[...] + _dot(a.astype(BF16), w_ref[...])


def _mlstm_out(hs, proj, hw, w, x):
    n = x.shape[0]
    tm = TOKEN_TILE
    o_block = (2 * M_HK + M_HV) // M_HV
    return pl.pallas_call(
        _mlstm_out_kernel,
        grid=(n // tm,),
        in_specs=[
            pl.BlockSpec((tm, M_HV), lambda i: (i, 0)),
            pl.BlockSpec((tm, M_HV), lambda i: (i, o_block)),
            _resident((1, M_HV)),
            _resident((M_HV, D_MODEL)),
            pl.BlockSpec((tm, D_MODEL), lambda i: (i, 0)),
        ],
        out_specs=pl.BlockSpec((tm, D_MODEL), lambda i: (i, 0)),
        out_shape=jax.ShapeDtypeStruct((n, D_MODEL), F32),
        compiler_params=_params("parallel"),
        name="mlstm_out",
    )(hs, proj, hw, w, x)


def _head_norm_rope(y, w, cos, s_lo, s_hi):
    lane = lax.broadcasted_iota(jnp.int32, y.shape, 1)
    first = lane < S_HD
    sq = y * y
    lo = jnp.sum(jnp.where(first, sq, 0.0), axis=-1, keepdims=True)
    hi = jnp.sum(jnp.where(first, 0.0, sq), axis=-1, keepdims=True)
    ms = jnp.where(first, lo, hi) * (1.0 / S_HD)
    z = y * lax.rsqrt(ms + EPS) * w
    half = ROT_DIM // 2
    return z * cos + pltpu.roll(z, LANES - half, 1) * s_lo + pltpu.roll(z, half, 1) * s_hi


def _swa_qkv_kernel(x_ref, nw_ref, w_ref, wvt_ref, qw_ref, kw_ref, cos_ref, slo_ref, shi_ref,
                    q_ref, k_ref, v_ref, vt_ref):
    hb = _rmsnorm(x_ref[...], nw_ref[...]).astype(BF16)
    cos, s_lo, s_hi = cos_ref[...], slo_ref[...], shi_ref[...]
    for j in range(S_HQ // LANES):
        cols = slice(j * LANES, (j + 1) * LANES)
        y = _head_norm_rope(_dot(hb, w_ref[:, cols]), qw_ref[...], cos, s_lo, s_hi)
        q_ref[:, cols] = (y * (S_HD ** -0.5)).astype(q_ref.dtype)
    for j in range(S_HKV // LANES):
        cols = slice(j * LANES, (j + 1) * LANES)
        wcols = slice(S_HQ + j * LANES, S_HQ + (j + 1) * LANES)
        k_ref[:, cols] = _head_norm_rope(_dot(hb, w_ref[:, wcols]), kw_ref[...], cos, s_lo, s_hi)
    v_ref[...] = _dot(hb, w_ref[:, S_HQ + S_HKV:])
    vt_ref[...] = _dot_nt(wvt_ref[...], hb).astype(vt_ref.dtype)


def _swa_qkv(x, nw, w, wvt, qw, kw, cos, s_lo, s_hi):
    n = x.shape[0]
    tm = TOKEN_TILE
    tab_blocks = cos.shape[0] // tm
    tab = pl.BlockSpec((tm, LANES), lambda i: (i % tab_blocks, 0))
    return pl.pallas_call(
        _swa_qkv_kernel,
        grid=(n // tm,),
        in_specs=[
            pl.BlockSpec((tm, D_MODEL), lambda i: (i, 0)),
            _resident((1, D_MODEL)),
            _resident((D_MODEL, S_HQ + 2 * S_HKV)),
            _resident((S_HKV, D_MODEL)),
            _resident((1, LANES)),
            _resident((1, LANES)),
            tab, tab, tab,
        ],
        out_specs=[
            pl.BlockSpec((tm, S_HQ), lambda i: (i, 0)),
            pl.BlockSpec((tm, S_HKV), lambda i: (i, 0)),
            pl.BlockSpec((tm, S_HKV), lambda i: (i, 0)),
            pl.BlockSpec((S_HKV, tm), lambda i: (0, i)),
        ],
        out_shape=[
            jax.ShapeDtypeStruct((n, S_HQ), BF16),
            jax.ShapeDtypeStruct((n, S_HKV), F32),
            jax.ShapeDtypeStruct((n, S_HKV), F32),
            jax.ShapeDtypeStruct((S_HKV, n), BF16),
        ],
        compiler_params=_params("parallel"),
        name="swa_qkv",
    )(x, nw, w, wvt, qw, kw, cos, s_lo, s_hi)


def _rope_tables(pos):
    half = ROT_DIM // 2
    inv = ROPE_THETA ** (-jnp.arange(half, dtype=F32) * 2.0 / ROT_DIM)
    ang = pos[:, None] * inv[None, :]
    cos, sin = jnp.cos(ang), jnp.sin(ang)
    n = pos.shape[0]
    pad = jnp.zeros((n, S_HD - ROT_DIM), F32)
    zero = jnp.zeros((n, half), F32)
    cos_h = jnp.concatenate([cos, cos, pad + 1.0], axis=1)
    lo_h = jnp.concatenate([-sin, zero, pad], axis=1)
    hi_h = jnp.concatenate([zero, sin, pad], axis=1)
    two = lambda t: jnp.concatenate([t, t], axis=1)
    return two(cos_h), two(lo_h), two(hi_h)


def _softmax_over_keys(s, mask, sink):
    s = jnp.where(mask, s, NEG_INF)
    m = jnp.maximum(jnp.max(s, axis=0, keepdims=True), sink)
    p = jnp.exp(s - m)
    denom = jnp.sum(p, axis=0, keepdims=True) + jnp.exp(sink - m)
    return p * (1.0 / denom)


def _swa_prompt_kernel(sink_ref, q_ref, kp_ref, ko_ref, vtp_ref, vto_ref, o_ref):
    blk = pl.program_id(1)
    W = WINDOW
    kf = jnp.concatenate([kp_ref[0], ko_ref[0]], axis=0)
    vt = jnp.concatenate([vtp_ref[...], vto_ref[...]], axis=1)
    j = lax.broadcasted_iota(jnp.int32, (2 * W, W), 0)
    rel = j - lax.broadcasted_iota(jnp.int32, (2 * W, W), 1)
    mask = (rel >= 0) & (rel <= W) & ((j >= W) | (blk > 0))
    low = lax.broadcasted_iota(jnp.int32, (2 * W, LANES), 1) < S_HD
    q = q_ref[0]
    halves = LANES // S_HD
    k_padded = {}
    for c in range(S_HKV // LANES):
        kc = kf[:, c * LANES:(c + 1) * LANES]
        kr = pltpu.roll(kc, S_HD, 1)
        for half in range(halves):
            k_lo = jnp.where(low, kc if half == 0 else kr, 0.0).astype(BF16)
            k_hi = jnp.where(low, 0.0, kr if half == 0 else kc).astype(BF16)
            k_padded[c * halves + half] = (k_lo, k_hi)

    def scores(h):
        pair = h // halves
        return _dot_nt(k_padded[h // S_GROUP][h % halves], q[:, pair * LANES:(pair + 1) * LANES])

    pending = [scores(h) for h in range(SCORE_LOOKAHEAD)]
    outs = []
    for h in range(S_HEADS):
        g = h // S_GROUP
        sink = sink_ref[h]
        s = jnp.where(mask, pending.pop(0), NEG_INF)
        m = jnp.maximum(jnp.max(s, axis=0, keepdims=True), sink)
        p = jnp.exp(s - m)
        denom = jnp.sum(p, axis=0, keepdims=True) + jnp.exp(sink - m)
        if h + SCORE_LOOKAHEAD < S_HEADS:
            pending.append(scores(h + SCORE_LOOKAHEAD))
        outs.append(_dot(vt[g * S_HD:(g + 1) * S_HD, :], p.astype(BF16)) * (1.0 / denom))
        if len(outs) == halves:
            pair = h // halves
            o_pair = jnp.concatenate(outs, axis=0).T
            o_ref[0, :, pair * LANES:(pair + 1) * LANES] = o_pair.astype(o_ref.dtype)
            outs = []


def _swa_prompt(q, k, vt, sinks, B, T):
    W = WINDOW
    nb = T // W
    prev = lambda b, i: (b, jnp.maximum(i - 1, 0), 0)
    own = lambda b, i: (b, i, 0)
    return pl.pallas_call(
        _swa_prompt_kernel,
        grid=(B, nb),
        in_specs=[
            pl.BlockSpec(memory_space=pltpu.SMEM),
            pl.BlockSpec((1, W, S_HQ), own),
            pl.BlockSpec((1, W, S_HKV), prev),
            pl.BlockSpec((1, W, S_HKV), own),
            pl.BlockSpec((S_HKV, W), lambda b, i: (0, b * nb + jnp.maximum(i - 1, 0))),
            pl.BlockSpec((S_HKV, W), lambda b, i: (0, b * nb + i)),
        ],
        out_specs=pl.BlockSpec((1, W, S_HQ), own),
        out_shape=jax.ShapeDtypeStruct((B, T, S_HQ), BF16),
        compiler_params=_params("parallel", "parallel"),
        name="swa_prompt",
    )(sinks, q, k, k, vt, vt)


SAMPLE_KEY_PAD = 16


def _swa_sample_kernel(q_ref, kn_ref, vn_ref, kc_ref, vc_ref, sink_ref, o_ref, ko_ref, vo_ref, *, T):
    nb, W = kc_ref.shape[0], kc_ref.shape[1]
    J = W + SAMPLE_KEY_PAD
    j = lax.broadcasted_iota(jnp.int32, (J, LANES), 0)
    t = lax.rem(lax.broadcasted_iota(jnp.int32, (J, LANES), 1), T)
    rel = t + W - j
    mask = (rel >= 0) & (rel <= WINDOW)
    low = lax.broadcasted_iota(jnp.int32, (T, LANES), 1) < S_HD
    zero = jnp.zeros((T, LANES), F32)
    kpad = jnp.zeros((SAMPLE_KEY_PAD - T, S_HKV), F32)
    sink = sink_ref[...]
    halves = LANES // S_HD
    for b in range(nb):
        rows = slice(b * T, (b + 1) * T)
        kn, vn, kc, vc = kn_ref[rows, :], vn_ref[rows, :], kc_ref[b], vc_ref[b]
        ko_ref[b, :W - T] = kc[T:]
        ko_ref[b, W - T:] = kn
        vo_ref[b, :W - T] = vc[T:]
        vo_ref[b, W - T:] = vn
        kall = jnp.concatenate([kc, kn, kpad], axis=0).astype(BF16)
        vall = jnp.concatenate([vc, vn, kpad], axis=0).astype(BF16)
        qf = q_ref[rows, :].astype(F32)
        blocks = []
        for h in range(S_HEADS):
            g = h // S_GROUP
            src = qf[:, (h // halves) * LANES:(h // halves + 1) * LANES]
            if h % halves != g % halves:
                src = pltpu.roll(src, S_HD, 1)
            src = jnp.where(low if g % halves == 0 else ~low, src, 0.0)
            blocks.append(jnp.concatenate(
                [src if c == g // halves else zero for c in range(S_HKV // LANES)], axis=1))
        qblk = jnp.concatenate(blocks, axis=0).astype(BF16)
        pn = _softmax_over_keys(_dot_nt(kall, qblk), mask, sink)
        o_full = _dot_tn(pn.astype(BF16), vall)
        for pair in range(S_HEADS // halves):
            pieces = []
            for hh in range(halves):
                h = pair * halves + hh
                g = h // S_GROUP
                piece = o_full[h * T:(h + 1) * T, (g // halves) * LANES:(g // halves + 1) * LANES]
                if g % halves != hh:
                    piece = pltpu.roll(piece, S_HD, 1)
                pieces.append(piece)
            o_ref[rows, pair * LANES:(pair + 1) * LANES] = jnp.where(low, pieces[0], pieces[1]).astype(o_ref.dtype)


def _swa_sample(q, kn, vn, kc, vc, layer, sinks, B, T):
    nb = SAMPLE_BLOCK
    W = kc.shape[2]
    rows = nb * T
    assert S_HEADS * T == LANES and LANES // S_HD == 2 and T <= SAMPLE_KEY_PAD
    return pl.pallas_call(
        functools.partial(_swa_sample_kernel, T=T),
        grid=(B // nb,),
        in_specs=[
            pl.BlockSpec((rows, S_HQ), lambda i: (i, 0)),
            pl.BlockSpec((rows, S_HKV), lambda i: (i, 0)),
            pl.BlockSpec((rows, S_HKV), lambda i: (i, 0)),
            pl.BlockSpec((None, nb, W, S_HKV), lambda i: (layer, i, 0, 0)),
            pl.BlockSpec((None, nb, W, S_HKV), lambda i: (layer, i, 0, 0)),
            _resident((1, LANES)),
        ],
        out_specs=[
            pl.BlockSpec((rows, S_HQ), lambda i: (i, 0)),
            pl.BlockSpec((nb, W, S_HKV), lambda i: (i, 0, 0)),
            pl.BlockSpec((nb, W, S_HKV), lambda i: (i, 0, 0)),
        ],
        out_shape=[
            jax.ShapeDtypeStruct((B * T, S_HQ), BF16),
            jax.ShapeDtypeStruct((B, W, S_HKV), F32),
            jax.ShapeDtypeStruct((B, W, S_HKV), F32),
        ],
        compiler_params=_params("parallel"),
        name="swa_sample",
    )(q, kn, vn, kc, vc, sinks)


def _proj_residual_kernel(a_ref, w_ref, x_ref, y_ref):
    y_ref[...] = x_ref[...] + _dot(a_ref[...], w_ref[...])


def _proj_residual(a, w, x):
    n = x.shape[0]
    tm = TOKEN_TILE
    return pl.pallas_call(
        _proj_residual_kernel,
        grid=(n // tm,),
        in_specs=[
            pl.BlockSpec((tm, a.shape[1]), lambda i: (i, 0)),
            _resident(w.shape),
            pl.BlockSpec((tm, D_MODEL), lambda i: (i, 0)),
        ],
        out_specs=pl.BlockSpec((tm, D_MODEL), lambda i: (i, 0)),
        out_shape=jax.ShapeDtypeStruct((n, D_MODEL), F32),
        compiler_params=_params("parallel"),
        name="proj_residual",
    )(a, w, x)


def _gelu(x):
    return 0.5 * x * (1.0 + lax.erf(x * (2.0 ** -0.5)))


def _convglu_body(x_ref, nw_ref, wg_ref, wu_ref, cw_ref, cb_ref, wd_ref, y_ref, act_ref, prev_rows,
                  mixer=None):
    y_ref[...] = x_ref[...] if mixer is None else x_ref[...] + _dot(mixer[0][...], mixer[1][...])
    x_ref = y_ref
    hb = _rmsnorm(x_ref[...], nw_ref[...]).astype(BF16)
    for f in range(D_FF // FF_CHUNK):
        cols = slice(f * FF_CHUNK, (f + 1) * FF_CHUNK)
        g = _dot(hb, wg_ref[:, cols])
        u = _dot(hb, wu_ref[:, cols])
        g1, g2 = prev_rows(cols, g)
        gc = cb_ref[:, cols] + cw_ref[0:1, cols] * g2 + cw_ref[1:2, cols] * g1 + cw_ref[2:3, cols] * g
        act_ref[:, cols] = (_gelu(gc) * u).astype(BF16)
    y_ref[...] = x_ref[...] + _dot(act_ref[...], wd_ref[...])


def _convglu_prompt_kernel(*refs, tiles_per_seq, fused_mixer):
    mixer = tuple(refs[:2]) if fused_mixer else None
    (x_ref, nw_ref, wg_ref, wu_ref, cw_ref, cb_ref, wd_ref,
     y_ref, g_ref, act_ref, tail_ref) = refs[2:] if fused_mixer else refs
    tm = x_ref.shape[0]
    row = lax.broadcasted_iota(jnp.int32, (tm, FF_CHUNK), 0)

    @pl.when(pl.program_id(0) % tiles_per_seq == 0)
    def _():
        tail_ref[...] = jnp.zeros_like(tail_ref)

    def prev_rows(cols, g):
        tail = tail_ref[:, cols]
        last, last2 = tail[SUBLANES - 1:SUBLANES, :], tail[SUBLANES - 2:SUBLANES - 1, :]
        g1 = jnp.where(row < 1, last, pltpu.roll(g, 1, 0))
        g2 = jnp.where(row < 2, jnp.where(row < 1, last2, last), pltpu.roll(g, 2, 0))
        tail_ref[:, cols] = g[tm - SUBLANES:, :]
        return g1, g2

    _convglu_body(x_ref, nw_ref, wg_ref, wu_ref, cw_ref, cb_ref, wd_ref, y_ref, act_ref, prev_rows, mixer)
    g_ref[0] = tail_ref[...]


def _convglu_sample_kernel(x_ref, nw_ref, wg_ref, wu_ref, cw_ref, cb_ref, wd_ref, e_ref,
                           y_ref, g_ref, act_ref, *, seq_len):
    tm = x_ref.shape[0]
    tpos = lax.broadcasted_iota(jnp.int32, (tm, FF_CHUNK), 0) % seq_len

    def prev_rows(cols, g):
        g_ref[:, cols] = g
        e = e_ref[:, cols]
        g1 = jnp.where(tpos < 1, pltpu.roll(e, tm - 1, 0), pltpu.roll(g, 1, 0))
        g2 = jnp.where(tpos < 2, e, pltpu.roll(g, 2, 0))
        return g1, g2

    _convglu_body(x_ref, nw_ref, wg_ref, wu_ref, cw_ref, cb_ref, wd_ref, y_ref, act_ref, prev_rows)


def _convglu_prompt(x, nw, wg, wu, cw, cb, wd, B, T, mixer=None):
    n = x.shape[0]
    tm = TOKEN_TILE
    assert T % tm == 0
    tiles_per_seq = T // tm
    mixer_specs = [] if mixer is None else [
        pl.BlockSpec((tm, mixer[0].shape[1]), lambda i: (i, 0)), _resident(mixer[1].shape)]

    return pl.pallas_call(
        functools.partial(_convglu_prompt_kernel, tiles_per_seq=tiles_per_seq, fused_mixer=mixer is not None),
        grid=(n // tm,),
        in_specs=mixer_specs + [
            pl.BlockSpec((tm, D_MODEL), lambda i: (i, 0)),
            _resident((1, D_MODEL)),
            _resident((D_MODEL, D_FF)),
            _resident((D_MODEL, D_FF)),
            _resident((CONV_W, D_FF)),
            _resident((1, D_FF)),
            _resident((D_FF, D_MODEL)),
        ],
        out_specs=[
            pl.BlockSpec((tm, D_MODEL), lambda i: (i, 0)),
            pl.BlockSpec((1, SUBLANES, D_FF), lambda i: (i // tiles_per_seq, 0, 0)),
        ],
        out_shape=[
            jax.ShapeDtypeStruct((n, D_MODEL), F32),
            jax.ShapeDtypeStruct((B, SUBLANES, D_FF), F32),
        ],
        scratch_shapes=[pltpu.VMEM((tm, D_FF), BF16), pltpu.VMEM((SUBLANES, D_FF), F32)],
        compiler_params=_params("arbitrary"),
        name="convglu_prompt",
    )(*(mixer or ()), x, nw, wg, wu, cw, cb, wd)


def _convglu_sample(x, nw, wg, wu, cw, cb, wd, e, T):
    n = x.shape[0]
    tm = TOKEN_TILE
    assert tm % T == 0 and T >= CONV_W - 1 and CONV_W == 3

    return pl.pallas_call(
        functools.partial(_convglu_sample_kernel, seq_len=T),
        grid=(n // tm,),
        in_specs=[
            pl.BlockSpec((tm, D_MODEL), lambda i: (i, 0)),
            _resident((1, D_MODEL)),
            _resident((D_MODEL, D_FF)),
            _resident((D_MODEL, D_FF)),
            _resident((CONV_W, D_FF)),
            _resident((1, D_FF)),
            _resident((D_FF, D_MODEL)),
            pl.BlockSpec((tm, D_FF), lambda i: (i, 0)),
        ],
        out_specs=[
            pl.BlockSpec((tm, D_MODEL), lambda i: (i, 0)),
            pl.BlockSpec((tm, D_FF), lambda i: (i, 0)),
        ],
        out_shape=[
            jax.ShapeDtypeStruct((n, D_MODEL), F32),
            jax.ShapeDtypeStruct((n, D_FF), F32),
        ],
        scratch_shapes=[pltpu.VMEM((tm, D_FF), BF16)],
        compiler_params=_params("parallel"),
        name="convglu_sample",
    )(x, nw, wg, wu, cw, cb, wd, e)


def kernel(x_prompt, x_sample, state_mlstm_C, state_mlstm_n, state_mlstm_m, cache_swa_k, cache_swa_v,
           state_ffn_conv, norm_mix, norm_ffn, w_mlstm_in, b_mlstm_gates, mlstm_head_norm, w_mlstm_out,
           w_swa_qkv, swa_q_norm, swa_k_norm, swa_sinks, w_swa_out, w_ffn_up, ffn_conv_w, ffn_conv_b,
           w_ffn_down):
    B, T, _ = x_prompt.shape
    Bs, Ts, _ = x_sample.shape
    W = cache_swa_k.shape[2]
    xp = x_prompt.reshape(B * T, D_MODEL)
    xs = x_sample.reshape(Bs * Ts, D_MODEL)
    n_gates = 2 * M_HEADS
    cache_k = cache_swa_k.reshape(cache_swa_k.shape[0], Bs, W, S_HKV)
    cache_v = cache_swa_v.reshape(cache_swa_v.shape[0], Bs, W, S_HKV)

    tab_p = _rope_tables(jnp.arange(T, dtype=F32))
    tab_s = tuple(jnp.tile(t, (TOKEN_TILE // Ts, 1))
                  for t in _rope_tables(PAST_LEN + jnp.arange(Ts, dtype=F32)))

    pC, pn, pm, pk, pv, pconv = [], [], [], [], [], []
    sC, sn, sm, sk, sv, sconv = [], [], [], [], [], []
    for i in range(DEPTH):
        j = i // 2
        nw = norm_mix[i][None]
        mixer_p = None
        if i % 2 == 0:
            w_in = jnp.pad(w_mlstm_in[j], ((0, 0), (0, LANES - n_gates))).astype(BF16)
            bg = jnp.pad(b_mlstm_gates[j], (0, LANES - n_gates))[None]
            hw = mlstm_head_norm[j][None]
            wo = w_mlstm_out[j].astype(BF16)

            proj, gates = _mlstm_inproj(xp, nw, w_in, bg, BF16)
            hs, C1, n1, m1 = _mlstm_prompt(proj.reshape(B, T, M_PROJ), gates.reshape(B, T, LANES), B, T)
            xp = _mlstm_out(hs.reshape(B * T, M_HV), proj, hw, wo, xp)
            pC.append(C1)
            pn.append(n1[:, :, 0])
            pm.append(m1[:, :, 0, 0])

            proj, gates = _mlstm_inproj(xs, nw, w_in, bg, F32)
            m0 = jnp.broadcast_to(state_mlstm_m[j][:, :, None, None], (Bs, M_HEADS, 1, LANES))
            hs, C2, n2, m2 = _mlstm_sample(proj, gates, state_mlstm_C, j, state_mlstm_n[j][:, :, None, :],
                                           m0, Bs, Ts)
            xs = _mlstm_out(hs, proj, hw, wo, xs)
            sC.append(C2)
            sn.append(n2[:, :, 0])
            sm.append(m2[:, :, 0, 0])
        else:
            w = w_swa_qkv[j].astype(BF16)
            qw = jnp.tile(swa_q_norm[j], LANES // S_HD)[None]
            kw = jnp.tile(swa_k_norm[j], LANES // S_HD)[None]
            wvt = w[:, S_HQ + S_HKV:].T
            sinks = swa_sinks[j]
            wo = w_swa_out[j].astype(BF16)

            q, k, v, vt = _swa_qkv(xp, nw, w, wvt, qw, kw, *tab_p)
            o = _swa_prompt(q.reshape(B, T, S_HQ), k.reshape(B, T, S_HKV), vt, sinks, B, T)
            mixer_p = (o.reshape(B * T, S_HQ), wo)
            pk.append(k.reshape(B, T, S_KV, S_HD)[:, T - WINDOW:])
            pv.append(v.reshape(B, T, S_KV, S_HD)[:, T - WINDOW:])

            q, k, v, _ = _swa_qkv(xs, nw, w, wvt, qw, kw, *tab_s)
            o, k2, v2 = _swa_sample(q, k, v, cache_k, cache_v, j, jnp.repeat(sinks, Ts)[None], Bs, Ts)
            xs = _proj_residual(o, wo, xs)
            sk.append(k2.reshape(Bs, W, S_KV, S_HD))
            sv.append(v2.reshape(Bs, W, S_KV, S_HD))

        nwf = norm_ffn[i][None]
        wg = w_ffn_up[i][:, :D_FF].astype(BF16)
        wu = w_ffn_up[i][:, D_FF:].astype(BF16)
        cw = ffn_conv_w[i]
        cb = ffn_conv_b[i][None]
        wd = w_ffn_down[i].astype(BF16)

        xp, gtail = _convglu_prompt(xp, nwf, wg, wu, cw, cb, wd, B, T, mixer_p)
        pconv.append(gtail[:, SUBLANES - (CONV_W - 1):])

        st = state_ffn_conv[i]
        e = jnp.pad(st, ((0, 0), (0, Ts - (CONV_W - 1)), (0, 0))).reshape(Bs * Ts, D_FF)
        xs, g = _convglu_sample(xs, nwf, wg, wu, cw, cb, wd, e, Ts)
        sconv.append(g.reshape(Bs, Ts, D_FF)[:, Ts - (CONV_W - 1):])

    return (xp.reshape(B, T, D_MODEL), xs.reshape(Bs, Ts, D_MODEL),
            jnp.stack(pC), jnp.stack(pn), jnp.stack(pm), jnp.stack(pk), jnp.stack(pv), jnp.stack(pconv),
            jnp.stack(sC), jnp.stack(sn), jnp.stack(sm), jnp.stack(sk), jnp.stack(sv), jnp.stack(sconv))
```
